```python
import math
import jax, jax.numpy as jnp
from jax import lax
import numpy as np

D_MODEL = 1024
BATCH = 8
SEQ = 4096
DEPTH = 2

HEAD_DIM = 64
A_HEADS = 4
A_WIDTH = A_HEADS * HEAD_DIM
A_PATTERNS = ((128, 1), (512, 4), (2048, 16))
B_HEADS = 6
B_KV_HEADS = 2
B_GROUP = B_HEADS // B_KV_HEADS
B_WIDTH = B_HEADS * HEAD_DIM
B_KV_WIDTH = B_KV_HEADS * HEAD_DIM
B_WINDOW = 128
B_BLOCK = 128
C_HEADS = 6
C_NOPE = 64
C_ROPE = 32
C_V = 64
C_WIDTH = C_HEADS * C_V
Q_LORA = 256
KV_LORA = 128
ROPE_THETA = 10000.0
C_QBLOCK = 128
D_MIX = A_WIDTH + B_WIDTH + C_WIDTH
IN_SPLITS = (A_WIDTH, A_WIDTH, A_WIDTH, A_WIDTH,
             B_WIDTH, B_KV_WIDTH, B_KV_WIDTH, B_WIDTH,
             Q_LORA, KV_LORA, C_ROPE, C_WIDTH)
D_IN = 4 * A_WIDTH + 2 * B_WIDTH + 2 * B_KV_WIDTH + Q_LORA + KV_LORA + C_ROPE + C_WIDTH
N_ALIBI = A_HEADS + B_HEADS
RMS_EPS = 1e-6
NEG_INF = -1e30

kernel_name = "hybrid_dilated_swa_mla_encoder"


def rmsnorm(x, g):
    xf = x.astype(jnp.float32)
    y = xf * lax.rsqrt(jnp.mean(xf * xf, axis=-1, keepdims=True) + RMS_EPS)
    return (y * g.astype(jnp.float32)).astype(x.dtype)


def split_cols(t, sizes):
    out, off = [], 0
    for s in sizes:
        out.append(t[..., off:off + s])
        off += s
    return out


def alibi_slopes():
    s = 2.0 ** (-8.0 * jnp.arange(1, N_ALIBI + 1, dtype=jnp.float32) / N_ALIBI)
    return s[B_HEADS:], s[:B_HEADS]


def rope(t, positions):
    half = C_ROPE // 2
    freq = ROPE_THETA ** (-2.0 * jnp.arange(half, dtype=jnp.float32) / C_ROPE)
    ang = positions.astype(jnp.float32)[..., None] * freq
    ang = ang.reshape(ang.shape[:2] + (1,) * (t.ndim - 3) + (half,))
    cos, sin = jnp.cos(ang), jnp.sin(ang)
    tf = t.astype(jnp.float32)
    t1, t2 = tf[..., :half], tf[..., half:]
    return jnp.concatenate([t1 * cos - t2 * sin, t1 * sin + t2 * cos], axis=-1).astype(t.dtype)


def _windows(t, blk, nb):
    return t


def banded_attention(q, k, v, half, blk, key_valid, dist_scale, slopes):
    *lead, hk, g, L, dh = q.shape
    nb = L // blk
    pad_kv = [(0, 0)] * (k.ndim - 2) + [(blk, blk), (0, 0)]
    kp = jnp.pad(k, pad_kv).reshape(k.shape[:-2] + (nb + 2, blk, dh))
    vp = jnp.pad(v, pad_kv).reshape(v.shape[:-2] + (nb + 2, blk, dh))
    kw = jnp.concatenate([kp[..., :-2, :, :], kp[..., 1:-1, :, :], kp[..., 2:, :, :]], axis=-2)
    vw = jnp.concatenate([vp[..., :-2, :, :], vp[..., 1:-1, :, :], vp[..., 2:, :, :]], axis=-2)
    valp = jnp.pad(key_valid, [(0, 0)] * (key_valid.ndim - 1) + [(blk, blk)], constant_values=False)
    valp = valp.reshape(key_valid.shape[:-1] + (nb + 2, blk))
    valw = jnp.concatenate([valp[..., :-2, :], valp[..., 1:-1, :], valp[..., 2:, :]], axis=-1)
    rel = jnp.arange(3 * blk)[None, :] - blk - jnp.arange(blk)[:, None]
    band = jnp.abs(rel) <= half
    bias = -slopes.astype(jnp.float32)[:, :, None, None, None] * (jnp.abs(rel) * dist_scale).astype(jnp.float32)
    qb = q.reshape(tuple(lead) + (hk, g, nb, blk, dh))
    s = jnp.einsum('...hgnqd,...hnkd->...hgnqk', qb, kw).astype(jnp.float32) * (dh ** -0.5) + bias
    mask = band & valw[..., None, None, :, None, :]
    s = jnp.where(mask, s, NEG_INF)
    m = jnp.max(s, axis=-1)
    p = jnp.exp(s - m[..., None])
    l = jnp.sum(p, axis=-1)
    acc = jnp.einsum('...hgnqk,...hnkd->...hgnqd', p, vw.astype(jnp.float32))
    shp = tuple(lead) + (hk, g, L)
    return m.reshape(shp), l.reshape(shp), acc.reshape(shp + (dh,))


def dilated_mixture(q, k, v, slopes_a):
    B, S, H, dh = q.shape
    ms, ls, accs = [], [], []
    for window, d in A_PATTERNS:
        half = window // (2 * d)
        blk = half
        chunk = d * blk
        Sp = ((S + chunk - 1) // chunk) * chunk
        L = Sp // d

        def to_res(t):
            t = jnp.pad(t, [(0, 0), (0, Sp - S), (0, 0), (0, 0)])
            return t.reshape(B, L, d, H, dh).transpose(0, 2, 3, 1, 4)

        qr, kr, vr = to_res(q)[:, :, :, None], to_res(k), to_res(v)
        valid = (jnp.arange(L)[None, :] * d + jnp.arange(d)[:, None]) < S
        m, l, acc = banded_attention(qr, kr, vr, half, blk, valid, d, slopes_a[:, None])
        ms.append(m[:, :, :, 0].transpose(0, 3, 1, 2).reshape(B, Sp, H)[:, :S])
        ls.append(l[:, :, :, 0].transpose(0, 3, 1, 2).reshape(B, Sp, H)[:, :S])
        accs.append(acc[:, :, :, 0].transpose(0, 3, 1, 2, 4).reshape(B, Sp, H, dh)[:, :S])
    M = jnp.maximum(jnp.maximum(ms[0], ms[1]), ms[2])
    es = [jnp.exp(m - M) for m in ms]
    num = es[0][..., None] * accs[0] + es[1][..., None] * accs[1] + es[2][..., None] * accs[2]
    den = es[0] * ls[0] + es[1] * ls[1] + es[2] * ls[2]
    out = num / den[..., None]
    return out.reshape(B, S, H * dh).astype(q.dtype)


def windowed_gqa_sink(q, k, v, sink, slopes_b):
    B, S, _, dh = q.shape
    qg = q.reshape(B, S, B_KV_HEADS, B_GROUP, dh).transpose(0, 2, 3, 1, 4)
    kg = k.transpose(0, 2, 1, 3)
    vg = v.transpose(0, 2, 1, 3)
    valid = jnp.ones((S,), dtype=bool)
    m, l, acc = banded_attention(qg, kg, vg, B_WINDOW, B_BLOCK, valid, 1,
                                 slopes_b.reshape(B_KV_HEADS, B_GROUP))
    sk = sink.astype(jnp.float32).reshape(B_KV_HEADS, B_GROUP)[:, :, None]
    M = jnp.maximum(m, sk)
    e = jnp.exp(m - M)
    den = l * e + jnp.exp(sk - M)
    out = acc * (e / den)[..., None]
    return out.transpose(0, 3, 1, 2, 4).reshape(B, S, B_WIDTH).astype(q.dtype)


def mla(cq, ckv, kr, positions, q_norm_g, kv_norm_g, w_uq, w_ukv):
    B, S, _ = cq.shape
    q = (rmsnorm(cq, q_norm_g) @ w_uq).reshape(B, S, C_HEADS, C_NOPE + C_ROPE)
    q_nope = q[..., :C_NOPE]
    q_rope = rope(q[..., C_NOPE:], positions)
    kv = (rmsnorm(ckv, kv_norm_g) @ w_ukv).reshape(B, S, C_HEADS, C_NOPE + C_V)
    k_nope = kv[..., :C_NOPE]
    v32 = kv[..., C_NOPE:].astype(jnp.float32)
    k_rope = rope(kr, positions)
    nq = S // C_QBLOCK
    qn = q_nope.reshape(B, nq, C_QBLOCK, C_HEADS, C_NOPE).swapaxes(0, 1)
    qr = q_rope.reshape(B, nq, C_QBLOCK, C_HEADS, C_ROPE).swapaxes(0, 1)
    scale = (C_NOPE + C_ROPE) ** -0.5

    def block(args):
        qn_b, qr_b = args
        s = (jnp.einsum('bqhd,bkhd->bhqk', qn_b, k_nope)
             + jnp.einsum('bqhd,bkd->bhqk', qr_b, k_rope)).astype(jnp.float32) * scale
        p = jax.nn.softmax(s, axis=-1)
        return jnp.einsum('bhqk,bkhd->bqhd', p, v32)

    o = lax.map(block, (qn, qr))
    return o.swapaxes(0, 1).reshape(B, S, C_WIDTH).astype(cq.dtype)


def setup_inputs(seed: int = 0) -> dict:
    key = jax.random.key(seed)
    ks = jax.random.split(key, 13)
    f32 = jnp.float32
    x = jax.random.normal(ks[0], (BATCH, SEQ, D_MODEL), f32)
    offset = jax.random.randint(ks[1], (BATCH, 1), 0, 1024, dtype=jnp.int32)
    positions = (offset + jnp.arange(SEQ, dtype=jnp.int32)[None, :]).astype(jnp.int32)
    pre_norm = 1.0 + 0.02 * jax.random.normal(ks[2], (DEPTH, D_MODEL), f32)
    w_in = jax.random.normal(ks[3], (DEPTH, D_MODEL, D_IN), f32) * D_MODEL ** -0.5
    q_a_norm = 1.0 + 0.02 * jax.random.normal(ks[4], (DEPTH, Q_LORA), f32)
    kv_a_norm = 1.0 + 0.02 * jax.random.normal(ks[5], (DEPTH, KV_LORA), f32)
    w_uq = jax.random.normal(ks[6], (DEPTH, Q_LORA, C_HEADS * (C_NOPE + C_ROPE)), f32) * Q_LORA ** -0.5
    w_ukv = jax.random.normal(ks[7], (DEPTH, KV_LORA, C_HEADS * (C_NOPE + C_V)), f32) * KV_LORA ** -0.5
    sink = 0.5 * jax.random.normal(ks[8], (DEPTH, B_HEADS), f32)
    w_o = jax.random.normal(ks[9], (DEPTH, D_MIX, D_MODEL), f32) * D_MIX ** -0.5
    post_norm = 1.0 + 0.02 * jax.random.normal(ks[10], (DEPTH, D_MODEL), f32)
    return {"x": x, "positions": positions, "pre_norm": pre_norm, "w_in": w_in,
            "q_a_norm": q_a_norm, "kv_a_norm": kv_a_norm, "w_uq": w_uq, "w_ukv": w_ukv,
            "sink": sink, "w_o": w_o, "post_norm": post_norm}


def reference(x, positions, pre_norm, w_in, q_a_norm, kv_a_norm, w_uq, w_ukv, sink, w_o, post_norm):
    B, S, _ = x.shape
    slopes_a, slopes_b = alibi_slopes()
    for i in range(DEPTH):
        h = rmsnorm(x, pre_norm[i])
        proj = h @ w_in[i]
        (qa, ka, va, ga, qb, kb, vb, gb, cq, ckv, kr, gc) = split_cols(proj, IN_SPLITS)
        ya = dilated_mixture(qa.reshape(B, S, A_HEADS, HEAD_DIM), ka.reshape(B, S, A_HEADS, HEAD_DIM),
                             va.reshape(B, S, A_HEADS, HEAD_DIM), slopes_a) * jax.nn.silu(ga)
        yb = windowed_gqa_sink(qb.reshape(B, S, B_HEADS, HEAD_DIM), kb.reshape(B, S, B_KV_HEADS, HEAD_DIM),
                               vb.reshape(B, S, B_KV_HEADS, HEAD_DIM), sink[i], slopes_b) * jax.nn.silu(gb)
        yc = mla(cq, ckv, kr, positions, q_a_norm[i], kv_a_norm[i], w_uq[i], w_ukv[i]) * jax.nn.silu(gc)
        y = jnp.concatenate([ya, yb, yc], axis=-1) @ w_o[i]
        x = x + rmsnorm(y, post_norm[i])
    return x
```

```python
import functools
import math

import numpy as np
import jax
import jax.numpy as jnp
from jax import lax
from jax.experimental import pallas as pl
from jax.experimental.pallas import tpu as pltpu

D_MODEL = 1024
HEAD_DIM = 64
A_HEADS = 4
A_WIDTH = A_HEADS * HEAD_DIM
A_PATTERNS = ((128, 1), (512, 4), (2048, 16))
A_HALF = 64
B_HEADS = 6
B_KV_HEADS = 2
B_GROUP = B_HEADS // B_KV_HEADS
B_WIDTH = B_HEADS * HEAD_DIM
B_KV_WIDTH = B_KV_HEADS * HEAD_DIM
B_WINDOW = 128
C_HEADS = 6
C_NOPE = 64
C_ROPE = 32
C_V = 64
C_WIDTH = C_HEADS * C_V
Q_LORA = 256
KV_LORA = 128
ROPE_THETA = 10000.0
D_MIX = A_WIDTH + B_WIDTH + C_WIDTH
N_ALIBI = A_HEADS + B_HEADS
RMS_EPS = 1e-6
NEG_INF = -1e30

LANES = 128
C_PAD = LANES
C_VROWS = 80
C_PAIR = 2

ROW_TILE = 512
A_QBLOCK = 128
A_KWIN = A_QBLOCK + 2 * A_HALF
B_QBLOCK = 128
B_KWIN = B_QBLOCK + 2 * B_WINDOW
C_QBLOCK = 256
VMEM_LIMIT = 48 * 1024 * 1024

B_HEAD_ORDER = (0, 3, 1, 4, 2, 5)

_F32 = jnp.float32
_BF16 = jnp.bfloat16


def _alibi_slopes():
    s = 2.0 ** (-8.0 * np.arange(1, N_ALIBI + 1, dtype=np.float64) / N_ALIBI)
    return [float(v) for v in s[B_HEADS:]], [float(v) for v in s[:B_HEADS]]


def _rms(x, g):
    return x * lax.rsqrt(jnp.mean(x * x, axis=-1, keepdims=True) + RMS_EPS) * g


def _dot(a, b):
    return jnp.dot(a, b, preferred_element_type=_F32)


def _dot_nt(a, b):
    return lax.dot_general(a, b, (((1,), (1,)), ((), ())), preferred_element_type=_F32)


def _rope_table_kernel(pos_ref, freq_ref, cos_ref, sin_ref):
    ang = pos_ref[0].astype(_F32) * freq_ref[...]
    cos_ref[0] = jnp.cos(ang)
    sin_ref[0] = jnp.sin(ang)


def _rope_tables(positions):
    B, S = positions.shape
    half = C_ROPE // 2
    freq = ROPE_THETA ** (-2.0 * jnp.arange(half, dtype=_F32) / C_ROPE)
    lane_freq = jnp.zeros((1, C_PAD), _F32).at[0, C_NOPE:C_NOPE + C_ROPE].set(jnp.concatenate([freq, freq]))
    out = jax.ShapeDtypeStruct((B, S, C_PAD), _F32)
    return pl.pallas_call(
        _rope_table_kernel,
        out_shape=(out, out),
        grid=(B, S // ROW_TILE),
        in_specs=[pl.BlockSpec((1, ROW_TILE, 1), lambda b, i: (b, i, 0)),
                  pl.BlockSpec((1, C_PAD), lambda b, i: (0, 0))],
        out_specs=(pl.BlockSpec((1, ROW_TILE, C_PAD), lambda b, i: (b, i, 0)),
                   pl.BlockSpec((1, ROW_TILE, C_PAD), lambda b, i: (b, i, 0))),
        compiler_params=pltpu.CompilerParams(dimension_semantics=("arbitrary", "arbitrary")),
        name="rope_tables",
    )(positions.reshape(B, S, 1), lane_freq)


def _in_proj_kernel(x_ref, cos_ref, sin_ref, pre_g_ref, qn_g_ref, kvn_g_ref,
                    w_a_ref, w_g_ref, w_b_ref, w_c_ref, w_uq_ref, w_uqr_ref, w_uk_ref, w_uvt_ref,
                    qa_ref, ka_ref, va_ref, g_ref, qb_ref, kb_ref, vb_ref, qc_ref, kc_ref, vt_ref):
    h = _rms(x_ref[0], pre_g_ref[...]).astype(_BF16)
    score_scale = HEAD_DIM ** -0.5

    pa = _dot(h, w_a_ref[...])
    qa_ref[0] = (pa[:, :A_WIDTH] * score_scale).astype(_BF16)
    ka_ref[0] = pa[:, A_WIDTH:2 * A_WIDTH].astype(_BF16)
    va_ref[0] = pa[:, 2 * A_WIDTH:].astype(_BF16)

    for c in range(0, D_MIX, 256):
        g = _dot(h, w_g_ref[:, c:c + 256])
        g_ref[0, :, c:c + 256] = (g / (1.0 + jnp.exp(-g))).astype(_BF16)

    pb = _dot(h, w_b_ref[...])
    qb_ref[0] = (pb[:, :B_WIDTH] * score_scale).astype(_BF16)
    kb_ref[0] = pb[:, B_WIDTH:B_WIDTH + B_KV_WIDTH].astype(_BF16)
    vb_ref[0] = pb[:, B_WIDTH + B_KV_WIDTH:].astype(_BF16)

    pc = _dot(h, w_c_ref[...])
    cos = cos_ref[0]
    sin = sin_ref[0]
    cq = _rms(pc[:, :Q_LORA], qn_g_ref[...]).astype(_BF16)
    ckv = _rms(pc[:, Q_LORA:Q_LORA + KV_LORA], kvn_g_ref[...]).astype(_BF16)
    k_rope = pc[:, Q_LORA + KV_LORA:Q_LORA + KV_LORA + C_PAD] * cos + pc[:, Q_LORA + KV_LORA + C_PAD:] * sin
    c_scale = (C_NOPE + C_ROPE) ** -0.5
    for hd in range(C_HEADS):
        cols = slice(hd * C_PAD, (hd + 1) * C_PAD)
        q = _dot(cq, w_uq_ref[:, cols]) * cos + _dot(cq, w_uqr_ref[:, cols]) * sin
        qc_ref[0, hd] = (q * c_scale).astype(_BF16)
        kc_ref[0, hd] = (_dot(ckv, w_uk_ref[:, cols]) + k_rope).astype(_BF16)
    vt = _dot_nt(w_uvt_ref[...], ckv)
    row = lax.broadcasted_iota(jnp.int32, vt.shape, 0)
    ones_row = functools.reduce(jnp.logical_or, [row == hd * C_VROWS + C_V for hd in range(C_HEADS)])
    vt = jnp.where(ones_row, 1.0, vt).astype(_BF16)
    for hd in range(C_HEADS):
        vt_ref[0, hd, 0] = vt[hd * C_VROWS:(hd + 1) * C_VROWS]


def _in_proj(x, cos_t, sin_t, pre_g, qn_g, kvn_g, w_a, w_g, w_b, w_c, w_uq, w_uqr, w_uk, w_uvt):
    B, S, _ = x.shape
    nblk = S // ROW_TILE
    row = lambda n: pl.BlockSpec((1, ROW_TILE, n), lambda b, i: (b, i, 0))
    full = lambda a: pl.BlockSpec(a.shape, lambda b, i: (0,) * a.ndim)
    bf = lambda n: jax.ShapeDtypeStruct((B, S, n), _BF16)
    head4 = pl.BlockSpec((1, C_HEADS, ROW_TILE, C_PAD), lambda b, i: (b, 0, i, 0))
    weights = (pre_g, qn_g, kvn_g, w_a, w_g, w_b, w_c, w_uq, w_uqr, w_uk, w_uvt)
    return pl.pallas_call(
        _in_proj_kernel,
        out_shape=(bf(A_WIDTH), bf(A_WIDTH), bf(A_WIDTH), bf(D_MIX), bf(B_WIDTH), bf(B_KV_WIDTH), bf(B_KV_WIDTH),
                   jax.ShapeDtypeStruct((B, C_HEADS, S, C_PAD), _BF16),
                   jax.ShapeDtypeStruct((B, C_HEADS, S, C_PAD), _BF16),
                   jax.ShapeDtypeStruct((B, C_HEADS, nblk, C_VROWS, ROW_TILE), _BF16)),
        grid=(B, nblk),
        in_specs=[row(D_MODEL), row(C_PAD), row(C_PAD)] + [full(w) for w in weights],
        out_specs=(row(A_WIDTH), row(A_WIDTH), row(A_WIDTH), row(D_MIX), row(B_WIDTH), row(B_KV_WIDTH),
                   row(B_KV_WIDTH), head4, head4,
                   pl.BlockSpec((1, C_HEADS, 1, C_VROWS, ROW_TILE), lambda b, i: (b, 0, i, 0, 0))),
        compiler_params=pltpu.CompilerParams(dimension_semantics=("arbitrary", "arbitrary"),
                                             vmem_limit_bytes=VMEM_LIMIT),
        name="in_proj",
    )(x, cos_t, sin_t, *weights)


def _dilated_kernel(q_ref, k_ref, v_ref, o_ref, lse_ref, *, length, dilation, slopes):
    nblk = length // A_QBLOCK
    lane_head = lax.broadcasted_iota(jnp.int32, (A_QBLOCK, A_WIDTH), 1) // HEAD_DIM
    rel0 = (lax.broadcasted_iota(jnp.int32, (A_QBLOCK, A_KWIN), 1)
            - lax.broadcasted_iota(jnp.int32, (A_QBLOCK, A_KWIN), 0))

    def block(i, carry):
        t0 = pl.multiple_of(i * A_QBLOCK, A_QBLOCK)
        start = pl.multiple_of(jnp.clip(t0 - A_HALF, 0, length - A_KWIN), A_HALF)
        q = q_ref[0, pl.ds(t0, A_QBLOCK), :]
        kw = k_ref[0, pl.ds(start, A_KWIN), :]
        vw = v_ref[0, pl.ds(start, A_KWIN), :]
        dist = jnp.abs(rel0 + (start - t0))
        in_band = dist <= A_HALF
        dist_f = dist.astype(_F32)
        out = jnp.zeros((A_QBLOCK, A_WIDTH), _F32)
        lse = jnp.zeros((A_QBLOCK, A_WIDTH), _F32)
        for hd in range(A_HEADS):
            mine = lane_head == hd
            s = _dot_nt(jnp.where(mine, q, jnp.zeros_like(q)), kw)
            s = jnp.where(in_band, s - (slopes[hd] * dilation) * dist_f, NEG_INF)
            m = jnp.max(s, axis=-1, keepdims=True)
            p = jnp.exp(s - m)
            l = jnp.sum(p, axis=-1, keepdims=True)
            acc = _dot(p.astype(_BF16), vw)
            out = jnp.where(mine, acc / l, out)
            lse = jnp.where(mine, m + jnp.log(l), lse)
        o_ref[0, pl.ds(t0, A_QBLOCK), :] = out.astype(o_ref.dtype)
        lse_ref[0, pl.ds(t0, A_QBLOCK), :] = lse
        return carry

    lax.fori_loop(0, nblk, block, 0)


def _dilated_pattern(qa, ka, va, dilation, slopes):
    B, S, _ = qa.shape
    length = S // dilation
    view = lambda t: t.reshape(B, length, dilation * A_WIDTH)
    spec = pl.BlockSpec((1, length, A_WIDTH), lambda b, r: (b, 0, r))
    o, lse = pl.pallas_call(
        functools.partial(_dilated_kernel, length=length, dilation=dilation, slopes=slopes),
        out_shape=(jax.ShapeDtypeStruct((B, length, dilation * A_WIDTH), _BF16),
                   jax.ShapeDtypeStruct((B, length, dilation * A_WIDTH), _F32)),
        grid=(B, dilation),
        in_specs=[spec, spec, spec],
        out_specs=(spec, spec),
        compiler_params=pltpu.CompilerParams(dimension_semantics=("arbitrary", "arbitrary"),
                                             vmem_limit_bytes=VMEM_LIMIT),
        name=f"dilated_d{dilation}",
    )(view(qa), view(ka), view(va))
    return o.reshape(B, S, A_WIDTH), lse.reshape(B, S, A_WIDTH)


def _windowed_kernel(sink_ref, q_ref, k_ref, v_ref, o_ref, *, seq, slopes):
    i = pl.program_id(1)
    t0 = pl.multiple_of(i * B_QBLOCK, B_QBLOCK)
    start = pl.multiple_of(jnp.clip(t0 - B_WINDOW, 0, seq - B_KWIN), B_WINDOW)
    kw = k_ref[0, pl.ds(start, B_KWIN), :]
    vw = v_ref[0, pl.ds(start, B_KWIN), :]
    rel0 = (lax.broadcasted_iota(jnp.int32, (B_QBLOCK, B_KWIN), 1)
            - lax.broadcasted_iota(jnp.int32, (B_QBLOCK, B_KWIN), 0))
    dist = jnp.abs(rel0 + (start - t0))
    in_band = dist <= B_WINDOW
    dist_f = dist.astype(_F32)
    low = lax.broadcasted_iota(jnp.int32, (B_QBLOCK, LANES), 1) < HEAD_DIM
    for pair in range(B_HEADS // 2):
        q = q_ref[0, :, pair * LANES:(pair + 1) * LANES]
        outs = []
        for side in range(2):
            head = B_HEAD_ORDER[2 * pair + side]
            mine = low if side == 0 else jnp.logical_not(low)
            s = _dot_nt(jnp.where(mine, q, jnp.zeros_like(q)), kw)
            s = jnp.where(in_band, s - slopes[head] * dist_f, NEG_INF)
            m = jnp.max(s, axis=-1, keepdims=True)
            p = jnp.exp(s - m)
            l = jnp.sum(p, axis=-1, keepdims=True)
            acc = _dot(p.astype(_BF16), vw)
            sk = sink_ref[head]
            big = jnp.maximum(m, sk)
            e = jnp.exp(m - big)
            den = l * e + jnp.exp(sk - big)
            outs.append(acc * (e / den))
        o_ref[0, :, pair * LANES:(pair + 1) * LANES] = jnp.where(low, outs[0], outs[1]).astype(o_ref.dtype)


def _windowed(qb, kb, vb, sink, slopes):
    B, S, _ = qb.shape
    kv_spec = pl.BlockSpec((1, S, B_KV_WIDTH), lambda b, i, sink: (b, 0, 0))
    q_spec = pl.BlockSpec((1, B_QBLOCK, B_WIDTH), lambda b, i, sink: (b, i, 0))
    return pl.pallas_call(
        functools.partial(_windowed_kernel, seq=S, slopes=slopes),
        out_shape=jax.ShapeDtypeStruct((B, S, B_WIDTH), _BF16),
        grid_spec=pltpu.PrefetchScalarGridSpec(
            num_scalar_prefetch=1, grid=(B, S // B_QBLOCK),
            in_specs=[q_spec, kv_spec, kv_spec], out_specs=q_spec),
        compiler_params=pltpu.CompilerParams(dimension_semantics=("arbitrary", "arbitrary"),
                                             vmem_limit_bytes=VMEM_LIMIT),
        name="windowed_gqa",
    )(sink, qb, kb, vb)


def _latent_kernel(q_ref, k_ref, vt_ref, o_ref, *, seq):
    nkb = seq // ROW_TILE
    qs = [q_ref[0, hd] for hd in range(C_PAIR)]

    def step(j, carry):
        k0 = pl.multiple_of(j * ROW_TILE, ROW_TILE)
        new = []
        for hd in range(C_PAIR):
            m, acc = carry[hd]
            s = _dot_nt(k_ref[0, hd, pl.ds(k0, ROW_TILE), :], qs[hd])
            m_new = jnp.maximum(m, jnp.max(s, axis=0, keepdims=True))
            alpha = jnp.exp(m - m_new)
            p = jnp.exp(s - m_new).astype(_BF16)
            acc = alpha * acc + _dot(vt_ref[0, hd, j], p)
            new.append((m_new, acc))
        return tuple(new)

    init = tuple((jnp.full((1, C_QBLOCK), NEG_INF, _F32), jnp.zeros((C_VROWS, C_QBLOCK), _F32))
                 for _ in range(C_PAIR))
    final = lax.fori_loop(0, nkb, step, init)
    halves = []
    for hd in range(C_PAIR):
        acc = final[hd][1]
        acc = jnp.concatenate([acc, jnp.zeros((LANES - C_VROWS, C_QBLOCK), _F32)], axis=0)
        acc_t = acc.T
        halves.append(acc_t[:, :C_V] / acc_t[:, C_V:C_V + 1])
    o_ref[0] = jnp.concatenate(halves, axis=-1).astype(o_ref.dtype)


def _latent(qc, kc, vt):
    B, H, S, _ = qc.shape
    nkb = S // ROW_TILE
    return pl.pallas_call(
        functools.partial(_latent_kernel, seq=S),
        out_shape=jax.ShapeDtypeStruct((B, S, C_WIDTH), _BF16),
        grid=(B, H // C_PAIR, S // C_QBLOCK),
        in_specs=[pl.BlockSpec((1, C_PAIR, C_QBLOCK, C_PAD), lambda b, g, i: (b, g, i, 0)),
                  pl.BlockSpec((1, C_PAIR, S, C_PAD), lambda b, g, i: (b, g, 0, 0)),
                  pl.BlockSpec((1, C_PAIR, nkb, C_VROWS, ROW_TILE), lambda b, g, i: (b, g, 0, 0, 0))],
        out_specs=pl.BlockSpec((1, C_QBLOCK, C_PAIR * C_V), lambda b, g, i: (b, i, g)),
        compiler_params=pltpu.CompilerParams(dimension_semantics=("arbitrary", "arbitrary", "arbitrary"),
                                             vmem_limit_bytes=VMEM_LIMIT),
        name="latent_attention",
    )(qc, kc, vt)


def _out_kernel(x_ref, g_ref, o1_ref, o2_ref, o3_ref, l1_ref, l2_ref, l3_ref, yb_ref, yc_ref,
                w_o_ref, post_g_ref, out_ref, y_scr):
    l1, l2, l3 = l1_ref[0], l2_ref[0], l3_ref[0]
    big = jnp.maximum(jnp.maximum(l1, l2), l3)
    e1, e2, e3 = jnp.exp(l1 - big), jnp.exp(l2 - big), jnp.exp(l3 - big)
    num = e1 * o1_ref[0].astype(_F32) + e2 * o2_ref[0].astype(_F32) + e3 * o3_ref[0].astype(_F32)
    ya = num / (e1 + e2 + e3)
    y_scr[:, :A_WIDTH] = (ya * g_ref[0, :, :A_WIDTH].astype(_F32)).astype(_BF16)
    y_scr[:, A_WIDTH:A_WIDTH + B_WIDTH] = yb_ref[0] * g_ref[0, :, A_WIDTH:A_WIDTH + B_WIDTH]
    y_scr[:, A_WIDTH + B_WIDTH:] = yc_ref[0] * g_ref[0, :, A_WIDTH + B_WIDTH:]
    y = _dot(y_scr[...], w_o_ref[...])
    out_ref[0] = x_ref[0] + _rms(y, post_g_ref[...])


def _out_proj(x, gates, o_pats, lse_pats, yb, yc, w_o, post_g):
    B, S, _ = x.shape
    row = lambda n: pl.BlockSpec((1, ROW_TILE, n), lambda b, i: (b, i, 0))
    full = lambda a: pl.BlockSpec(a.shape, lambda b, i: (0,) * a.ndim)
    return pl.pallas_call(
        _out_kernel,
        out_shape=jax.ShapeDtypeStruct(x.shape, x.dtype),
        grid=(B, S // ROW_TILE),
        in_specs=[row(D_MODEL), row(D_MIX)] + [row(A_WIDTH)] * 6 + [row(B_WIDTH), row(C_WIDTH),
                                                                     full(w_o), full(post_g)],
        out_specs=row(D_MODEL),
        scratch_shapes=[pltpu.VMEM((ROW_TILE, D_MIX), _BF16)],
        compiler_params=pltpu.CompilerParams(dimension_semantics=("arbitrary", "arbitrary"),
                                             vmem_limit_bytes=VMEM_LIMIT),
        name="out_proj",
    )(x, gates, *o_pats, *lse_pats, yb, yc, w_o, post_g)


def _pair_heads(t, axis):
    parts = jnp.split(t, B_HEADS, axis=axis)
    return jnp.concatenate([parts[h] for h in B_HEAD_ORDER], axis=axis)


def _rot_cols(w):
    half = C_ROPE // 2
    return jnp.concatenate([-w[..., half:], w[..., :half]], axis=-1)


def _layer_weights(w_in, w_uq, w_ukv, w_o):
    o = 0
    cols = {}
    for name, n in (("qa", A_WIDTH), ("ka", A_WIDTH), ("va", A_WIDTH), ("ga", A_WIDTH),
                    ("qb", B_WIDTH), ("kb", B_KV_WIDTH), ("vb", B_KV_WIDTH), ("gb", B_WIDTH),
                    ("cq", Q_LORA), ("ckv", KV_LORA), ("kr", C_ROPE), ("gc", C_WIDTH)):
        cols[name] = w_in[:, o:o + n]
        o += n
    w_a = jnp.concatenate([cols["qa"], cols["ka"], cols["va"]], axis=1)
    w_g = jnp.concatenate([cols["ga"], _pair_heads(cols["gb"], 1), cols["gc"]], axis=1)
    w_b = jnp.concatenate([_pair_heads(cols["qb"], 1), cols["kb"], cols["vb"]], axis=1)
    zeros = lambda n: jnp.zeros((D_MODEL, n), w_in.dtype)
    pad_rope = lambda w: jnp.concatenate([zeros(C_NOPE), w, zeros(C_PAD - C_NOPE - C_ROPE)], axis=1)
    w_c = jnp.concatenate([cols["cq"], cols["ckv"], pad_rope(cols["kr"]), pad_rope(_rot_cols(cols["kr"]))], axis=1)

    uq = w_uq.reshape(Q_LORA, C_HEADS, C_NOPE + C_ROPE)
    zq = lambda n: jnp.zeros((Q_LORA, C_HEADS, n), w_uq.dtype)
    uq_main = jnp.concatenate([uq, zq(C_PAD - C_NOPE - C_ROPE)], axis=-1).reshape(Q_LORA, C_HEADS * C_PAD)
    uq_rot = jnp.concatenate([zq(C_NOPE), _rot_cols(uq[..., C_NOPE:]), zq(C_PAD - C_NOPE - C_ROPE)],
                             axis=-1).reshape(Q_LORA, C_HEADS * C_PAD)
    ukv = w_ukv.reshape(KV_LORA, C_HEADS, C_NOPE + C_V)
    uk = jnp.concatenate([ukv[..., :C_NOPE], jnp.zeros((KV_LORA, C_HEADS, C_PAD - C_NOPE), w_ukv.dtype)],
                         axis=-1).reshape(KV_LORA, C_HEADS * C_PAD)
    uvt = jnp.transpose(ukv[..., C_NOPE:], (1, 2, 0))
    uvt = jnp.concatenate([uvt, jnp.zeros((C_HEADS, C_VROWS - C_V, KV_LORA), w_ukv.dtype)], axis=1)
    uvt = uvt.reshape(C_HEADS * C_VROWS, KV_LORA)

    w_o_p = jnp.concatenate([w_o[:A_WIDTH], _pair_heads(w_o[A_WIDTH:A_WIDTH + B_WIDTH], 0),
                             w_o[A_WIDTH + B_WIDTH:]], axis=0)
    bf = lambda t: t.astype(_BF16)
    return (bf(w_a), bf(w_g), bf(w_b), bf(w_c), bf(uq_main), bf(uq_rot), bf(uk), bf(uvt)), bf(w_o_p)


def kernel(x, positions, pre_norm, w_in, q_a_norm, kv_a_norm, w_uq, w_ukv, sink, w_o, post_norm):
    depth = w_in.shape[0]
    slopes_a, slopes_b = _alibi_slopes()
    cos_t, sin_t = _rope_tables(positions)
    for i in range(depth):
        in_w, w_o_p = _layer_weights(w_in[i], w_uq[i], w_ukv[i], w_o[i])
        qa, ka, va, gates, qb, kb, vb, qc, kc, vt = _in_proj(
            x, cos_t, sin_t, pre_norm[i][None], q_a_norm[i][None], kv_a_norm[i][None], *in_w)
        pats = [_dilated_pattern(qa, ka, va, d, slopes_a) for _, d in A_PATTERNS]
        yb = _windowed(qb, kb, vb, sink[i], slopes_b)
        yc = _latent(qc, kc, vt)
        x = _out_proj(x, gates, [p[0] for p in pats], [p[1] for p in pats], yb, yc, w_o_p, post_norm[i][None])
    return x
```

```python
import functools
import math

import numpy as np
import jax
import jax.numpy as jnp
from jax import lax
from jax.experimental import pallas as pl
from jax.experimental.pallas import tpu as pltpu

D_MODEL = 1024
HEAD_DIM = 64
A_HEADS = 4
A_WIDTH = A_HEADS * HEAD_DIM
A_PATTERNS = ((128, 1), (512, 4), (2048, 16))
A_HALF = 64
B_HEADS = 6
B_KV_HEADS = 2
B_GROUP = B_HEADS // B_KV_HEADS
B_WIDTH = B_HEADS * HEAD_DIM
B_KV_WIDTH = B_KV_HEADS * HEAD_DIM
B_WINDOW = 128
C_HEADS = 6
C_NOPE = 64
C_ROPE = 32
C_V = 64
C_WIDTH = C_HEADS * C_V
Q_LORA = 256
KV_LORA = 128
ROPE_THETA = 10000.0
D_MIX = A_WIDTH + B_WIDTH + C_WIDTH
N_ALIBI = A_HEADS + B_HEADS
RMS_EPS = 1e-6
NEG_INF = -1e30
LOG2E = math.log2(math.e)

LANES = 128
C_PAD = LANES
C_VROWS = 80
C_PAIR = 2

ROW_TILE = 512
A_QBLOCK = 128
A_KWIN = A_QBLOCK + 2 * A_HALF
B_QBLOCK = 256
B_KWIN = B_QBLOCK + 2 * B_WINDOW
C_QBLOCK = 512
C_KBLOCK = 256
VMEM_LIMIT = 48 * 1024 * 1024

B_HEAD_ORDER = (0, 3, 1, 4, 2, 5)

_F32 = jnp.float32
_BF16 = jnp.bfloat16


def _alibi_slopes():
    s = 2.0 ** (-8.0 * np.arange(1, N_ALIBI + 1, dtype=np.float64) / N_ALIBI)
    return [float(v) for v in s[B_HEADS:]], [float(v) for v in s[:B_HEADS]]


def _rms(x, g):
    return x * lax.rsqrt(jnp.mean(x * x, axis=-1, keepdims=True) + RMS_EPS) * g


def _dot(a, b):
    return jnp.dot(a, b, preferred_element_type=_F32)


def _dot_nt(a, b):
    return lax.dot_general(a, b, (((1,), (1,)), ((), ())), preferred_element_type=_F32)


def _rope_table_kernel(pos_ref, freq_ref, cos_ref, sin_ref):
    ang = pos_ref[0].astype(_F32) * freq_ref[...]
    cos_ref[0] = jnp.cos(ang)
    sin_ref[0] = jnp.sin(ang)


def _rope_tables(positions):
    B, S = positions.shape
    half = C_ROPE // 2
    freq = ROPE_THETA ** (-2.0 * jnp.arange(half, dtype=_F32) / C_ROPE)
    lane_freq = jnp.zeros((1, C_PAD), _F32).at[0, C_NOPE:C_NOPE + C_ROPE].set(jnp.concatenate([freq, freq]))
    out = jax.ShapeDtypeStruct((B, S, C_PAD), _F32)
    return pl.pallas_call(
        _rope_table_kernel,
        out_shape=(out, out),
        grid=(B, S // ROW_TILE),
        in_specs=[pl.BlockSpec((1, ROW_TILE, 1), lambda b, i: (b, i, 0)),
                  pl.BlockSpec((1, C_PAD), lambda b, i: (0, 0))],
        out_specs=(pl.BlockSpec((1, ROW_TILE, C_PAD), lambda b, i: (b, i, 0)),
                   pl.BlockSpec((1, ROW_TILE, C_PAD), lambda b, i: (b, i, 0))),
        compiler_params=pltpu.CompilerParams(dimension_semantics=("arbitrary", "arbitrary")),
        name="rope_tables",
    )(positions.reshape(B, S, 1), lane_freq)


def _in_proj_kernel(x_ref, cos_ref, sin_ref, pre_g_ref, qn_g_ref, kvn_g_ref,
                    w_a_ref, w_g_ref, w_b_ref, w_c_ref, w_uq_ref, w_uqr_ref, w_uk_ref, w_uvt_ref,
                    qa_ref, ka_ref, va_ref, g_ref, qb_ref, kb_ref, vb_ref, qc_ref, kc_ref, vt_ref):
    h = _rms(x_ref[0], pre_g_ref[...]).astype(_BF16)
    score_scale = HEAD_DIM ** -0.5 * LOG2E

    pa = _dot(h, w_a_ref[...])
    qa_ref[0] = (pa[:, :A_WIDTH] * score_scale).astype(_BF16)
    ka_ref[0] = pa[:, A_WIDTH:2 * A_WIDTH].astype(_BF16)
    va_ref[0] = pa[:, 2 * A_WIDTH:].astype(_BF16)

    for c in range(0, D_MIX, 256):
        g = _dot(h, w_g_ref[:, c:c + 256])
        g_ref[0, :, c:c + 256] = (g / (1.0 + jnp.exp(-g))).astype(_BF16)

    pb = _dot(h, w_b_ref[...])
    qb_ref[0] = (pb[:, :B_WIDTH] * score_scale).astype(_BF16)
    kb_ref[0] = pb[:, B_WIDTH:B_WIDTH + B_KV_WIDTH].astype(_BF16)
    vb_ref[0] = pb[:, B_WIDTH + B_KV_WIDTH:].astype(_BF16)

    pc = _dot(h, w_c_ref[...])
    cos = cos_ref[0]
    sin = sin_ref[0]
    cq = _rms(pc[:, :Q_LORA], qn_g_ref[...]).astype(_BF16)
    ckv = _rms(pc[:, Q_LORA:Q_LORA + KV_LORA], kvn_g_ref[...]).astype(_BF16)
    k_rope = pc[:, Q_LORA + KV_LORA:Q_LORA + KV_LORA + C_PAD] * cos + pc[:, Q_LORA + KV_LORA + C_PAD:] * sin
    c_scale = (C_NOPE + C_ROPE) ** -0.5 * LOG2E
    for hd in range(C_HEADS):
        cols = slice(hd * C_PAD, (hd + 1) * C_PAD)
        q = _dot(cq, w_uq_ref[:, cols]) * cos + _dot(cq, w_uqr_ref[:, cols]) * sin
        qc_ref[0, hd] = (q * c_scale).astype(_BF16)
        kc_ref[0, hd] = (_dot(ckv, w_uk_ref[:, cols]) + k_rope).astype(_BF16)
    vt = _dot_nt(w_uvt_ref[...], ckv)
    row = lax.broadcasted_iota(jnp.int32, vt.shape, 0)
    ones_row = functools.reduce(jnp.logical_or, [row == hd * C_VROWS + C_V for hd in range(C_HEADS)])
    vt = jnp.where(ones_row, 1.0, vt).astype(_BF16)
    for hd in range(C_HEADS):
        for c in range(ROW_TILE // C_KBLOCK):
            vt_ref[0, hd, c] = vt[hd * C_VROWS:(hd + 1) * C_VROWS, c * C_KBLOCK:(c + 1) * C_KBLOCK]


def _in_proj(x, cos_t, sin_t, pre_g, qn_g, kvn_g, w_a, w_g, w_b, w_c, w_uq, w_uqr, w_uk, w_uvt):
    B, S, _ = x.shape
    nblk = S // ROW_TILE
    row = lambda n: pl.BlockSpec((1, ROW_TILE, n), lambda b, i: (b, i, 0))
    full = lambda a: pl.BlockSpec(a.shape, lambda b, i: (0,) * a.ndim)
    bf = lambda n: jax.ShapeDtypeStruct((B, S, n), _BF16)
    head4 = pl.BlockSpec((1, C_HEADS, ROW_TILE, C_PAD), lambda b, i: (b, 0, i, 0))
    weights = (pre_g, qn_g, kvn_g, w_a, w_g, w_b, w_c, w_uq, w_uqr, w_uk, w_uvt)
    return pl.pallas_call(
        _in_proj_kernel,
        out_shape=(bf(A_WIDTH), bf(A_WIDTH), bf(A_WIDTH), bf(D_MIX), bf(B_WIDTH), bf(B_KV_WIDTH), bf(B_KV_WIDTH),
                   jax.ShapeDtypeStruct((B, C_HEADS, S, C_PAD), _BF16),
                   jax.ShapeDtypeStruct((B, C_HEADS, S, C_PAD), _BF16),
                   jax.ShapeDtypeStruct((B, C_HEADS, S // C_KBLOCK, C_VROWS, C_KBLOCK), _BF16)),
        grid=(B, nblk),
        in_specs=[row(D_MODEL), row(C_PAD), row(C_PAD)] + [full(w) for w in weights],
        out_specs=(row(A_WIDTH), row(A_WIDTH), row(A_WIDTH), row(D_MIX), row(B_WIDTH), row(B_KV_WIDTH),
                   row(B_KV_WIDTH), head4, head4,
                   pl.BlockSpec((1, C_HEADS, ROW_TILE // C_KBLOCK, C_VROWS, C_KBLOCK),
                                lambda b, i: (b, 0, i, 0, 0))),
        compiler_params=pltpu.CompilerParams(dimension_semantics=("arbitrary", "arbitrary"),
                                             vmem_limit_bytes=VMEM_LIMIT),
        name="in_proj",
    )(x, cos_t, sin_t, *weights)


def _dilated_kernel(q_ref, k_ref, v_ref, o_ref, lse_ref, bias_ref, *, length, dilation, slopes):
    nblk = length // A_QBLOCK
    lane_head = lax.shift_right_logical(lax.broadcasted_iota(jnp.int32, (A_QBLOCK, A_WIDTH), 1),
                                        int(math.log2(HEAD_DIM)))

    @pl.when((pl.program_id(0) == 0) & (pl.program_id(1) == 0))
    def _():
        rel0 = (lax.broadcasted_iota(jnp.int32, (A_QBLOCK, A_KWIN), 1)
                - lax.broadcasted_iota(jnp.int32, (A_QBLOCK, A_KWIN), 0))
        for case, shift in enumerate((0, -A_HALF, -2 * A_HALF)):
            dist = jnp.abs(rel0 + shift)
            for hd in range(A_HEADS):
                bias_ref[case, hd] = jnp.where(dist <= A_HALF,
                                               dist.astype(_F32) * (-slopes[hd] * dilation * LOG2E), NEG_INF)

    def load(blk):
        t0 = pl.multiple_of(blk * A_QBLOCK, A_QBLOCK)
        start = pl.multiple_of(jnp.clip(t0 - A_HALF, 0, length - A_KWIN), A_HALF)
        case = jnp.where(blk == 0, 0, jnp.where(blk == nblk - 1, 2, 1))
        q = q_ref[0, pl.ds(t0, A_QBLOCK), :]
        q_heads = jnp.concatenate([jnp.where(lane_head == hd, q, jnp.zeros_like(q)) for hd in range(A_HEADS)],
                                  axis=0)
        return t0, case, q_heads, k_ref[0, pl.ds(start, A_KWIN), :], v_ref[0, pl.ds(start, A_KWIN), :]

    def softmax(s_all, case):
        ps, ms, ls = [], [], []
        for hd in range(A_HEADS):
            s = s_all[hd * A_QBLOCK:(hd + 1) * A_QBLOCK] + bias_ref[case, hd]
            m = jnp.max(s, axis=-1, keepdims=True)
            p = jnp.exp2(s - m)
            ls.append(jnp.sum(p, axis=-1, keepdims=True))
            ms.append(m)
            ps.append(p.astype(_BF16))
        return jnp.concatenate(ps, axis=0), ms, ls

    def finish(t0, acc_all, ms, ls):
        out = jnp.zeros((A_QBLOCK, A_WIDTH), _F32)
        lse = jnp.zeros((A_QBLOCK, A_WIDTH), _F32)
        for hd in range(A_HEADS):
            mine = lane_head == hd
            out = jnp.where(mine, acc_all[hd * A_QBLOCK:(hd + 1) * A_QBLOCK] * (1.0 / ls[hd]), out)
            lse = jnp.where(mine, ms[hd] + jnp.log2(ls[hd]), lse)
        o_ref[0, pl.ds(t0, A_QBLOCK), :] = out.astype(o_ref.dtype)
        lse_ref[0, pl.ds(t0, A_QBLOCK), :] = lse

    def two_blocks(i, carry):
        blocks = [load(2 * i + j) for j in range(2)]
        scores = [_dot_nt(q_heads, kw) for _, _, q_heads, kw, _ in blocks]
        for (t0, case, _, _, vw), s_all in zip(blocks, scores):
            p_all, ms, ls = softmax(s_all, case)
            finish(t0, _dot(p_all, vw), ms, ls)
        return carry

    lax.fori_loop(0, nblk // 2, two_blocks, 0)


def _dilated_pattern(qa, ka, va, dilation, slopes):
    B, S, _ = qa.shape
    length = S // dilation
    view = lambda t: t.reshape(B, length, dilation * A_WIDTH)
    spec = pl.BlockSpec((1, length, A_WIDTH), lambda b, r: (b, 0, r))
    o, lse = pl.pallas_call(
        functools.partial(_dilated_kernel, length=length, dilation=dilation, slopes=slopes),
        out_shape=(jax.ShapeDtypeStruct((B, length, dilation * A_WIDTH), _BF16),
                   jax.ShapeDtypeStruct((B, length, dilation * A_WIDTH), _F32)),
        grid=(B, dilation),
        in_specs=[spec, spec, spec],
        out_specs=(spec, spec),
        scratch_shapes=[pltpu.VMEM((3, A_HEADS, A_QBLOCK, A_KWIN), _F32)],
        compiler_params=pltpu.CompilerParams(dimension_semantics=("arbitrary", "arbitrary"),
                                             vmem_limit_bytes=VMEM_LIMIT),
        name=f"dilated_d{dilation}",
    )(view(qa), view(ka), view(va))
    return o.reshape(B, S, A_WIDTH), lse.reshape(B, S, A_WIDTH)


def _windowed_kernel(sink_ref, q_ref, k_ref, v_ref, o_ref, bias_ref, *, seq, slopes):
    i = pl.program_id(1)
    last = pl.num_programs(1) - 1
    t0 = pl.multiple_of(i * B_QBLOCK, B_QBLOCK)
    start = pl.multiple_of(jnp.clip(t0 - B_WINDOW, 0, seq - B_KWIN), B_WINDOW)

    @pl.when((i <= 1) | (i == last))
    def _():
        rel0 = (lax.broadcasted_iota(jnp.int32, (B_QBLOCK, B_KWIN), 1)
                - lax.broadcasted_iota(jnp.int32, (B_QBLOCK, B_KWIN), 0))
        dist = jnp.abs(rel0 + (start - t0))
        for idx, head in enumerate(B_HEAD_ORDER):
            bias_ref[idx] = jnp.where(dist <= B_WINDOW, dist.astype(_F32) * (-slopes[head] * LOG2E), NEG_INF)

    kw = k_ref[0, pl.ds(start, B_KWIN), :]
    vw = v_ref[0, pl.ds(start, B_KWIN), :]
    low = lax.broadcasted_iota(jnp.int32, (B_QBLOCK, LANES), 1) < HEAD_DIM
    npair = B_HEADS // 2

    def scores(pair):
        q = q_ref[0, :, pair * LANES:(pair + 1) * LANES]
        zero = jnp.zeros_like(q)
        return _dot_nt(jnp.concatenate([jnp.where(low, q, zero), jnp.where(low, zero, q)], axis=0), kw)

    s_next = scores(0)
    for pair in range(npair):
        s_pair, s_next = s_next, (scores(pair + 1) if pair + 1 < npair else None)
        ps, factors = [], []
        for side in range(2):
            idx = 2 * pair + side
            s = s_pair[side * B_QBLOCK:(side + 1) * B_QBLOCK] + bias_ref[idx]
            m = jnp.max(s, axis=-1, keepdims=True)
            p = jnp.exp2(s - m)
            l = jnp.sum(p, axis=-1, keepdims=True)
            sk = sink_ref[B_HEAD_ORDER[idx]] * LOG2E
            big = jnp.maximum(m, sk)
            e = jnp.exp2(m - big)
            factors.append(e / (l * e + jnp.exp2(sk - big)))
            ps.append(p.astype(_BF16))
        acc = _dot(jnp.concatenate(ps, axis=0), vw)
        out = jnp.where(low, acc[:B_QBLOCK] * factors[0], acc[B_QBLOCK:] * factors[1])
        o_ref[0, :, pair * LANES:(pair + 1) * LANES] = out.astype(o_ref.dtype)


def _windowed(qb, kb, vb, sink, slopes):
    B, S, _ = qb.shape
    kv_spec = pl.BlockSpec((1, S, B_KV_WIDTH), lambda b, i, sink: (b, 0, 0))
    q_spec = pl.BlockSpec((1, B_QBLOCK, B_WIDTH), lambda b, i, sink: (b, i, 0))
    return pl.pallas_call(
        functools.partial(_windowed_kernel, seq=S, slopes=slopes),
        out_shape=jax.ShapeDtypeStruct((B, S, B_WIDTH), _BF16),
        grid_spec=pltpu.PrefetchScalarGridSpec(
            num_scalar_prefetch=1, grid=(B, S // B_QBLOCK),
            in_specs=[q_spec, kv_spec, kv_spec], out_specs=q_spec,
            scratch_shapes=[pltpu.VMEM((B_HEADS, B_QBLOCK, B_KWIN), _F32)]),
        compiler_params=pltpu.CompilerParams(dimension_semantics=("arbitrary", "arbitrary"),
                                             vmem_limit_bytes=VMEM_LIMIT),
        name="windowed_gqa",
    )(sink, qb, kb, vb)


def _latent_kernel(q_ref, k_ref, vt_ref, o_ref, *, seq):
    nkb = seq // C_KBLOCK
    halves = []
    for hd in range(C_PAIR):
        q = q_ref[0, hd]

        def scores(j):
            return _dot_nt(k_ref[0, hd, j * C_KBLOCK:(j + 1) * C_KBLOCK, :], q)

        m = jnp.full((1, C_QBLOCK), NEG_INF, _F32)
        acc = jnp.zeros((C_VROWS, C_QBLOCK), _F32)
        s_next = scores(0)
        for j in range(nkb):
            s, s_next = s_next, (scores(j + 1) if j + 1 < nkb else None)
            m_new = jnp.maximum(m, jnp.max(s, axis=0, keepdims=True))
            alpha = jnp.exp2(m - m_new)
            p = jnp.exp2(s - m_new).astype(_BF16)
            acc = alpha * acc + _dot(vt_ref[0, hd, j], p)
            m = m_new
        acc = jnp.concatenate([acc, jnp.zeros((LANES - C_VROWS, C_QBLOCK), _F32)], axis=0)
        acc_t = acc.T
        halves.append(acc_t[:, :C_V] / acc_t[:, C_V:C_V + 1])
    o_ref[0] = jnp.concatenate(halves, axis=-1).astype(o_ref.dtype)


def _latent(qc, kc, vt):
    B, H, S, _ = qc.shape
    nkb = S // C_KBLOCK
    return pl.pallas_call(
        functools.partial(_latent_kernel, seq=S),
        out_shape=jax.ShapeDtypeStruct((B, S, C_WIDTH), _BF16),
        grid=(B, H // C_PAIR, S // C_QBLOCK),
        in_specs=[pl.BlockSpec((1, C_PAIR, C_QBLOCK, C_PAD), lambda b, g, i: (b, g, i, 0)),
                  pl.BlockSpec((1, C_PAIR, S, C_PAD), lambda b, g, i: (b, g, 0, 0)),
                  pl.BlockSpec((1, C_PAIR, nkb, C_VROWS, C_KBLOCK), lambda b, g, i: (b, g, 0, 0, 0))],
        out_specs=pl.BlockSpec((1, C_QBLOCK, C_PAIR * C_V), lambda b, g, i: (b, i, g)),
        compiler_params=pltpu.CompilerParams(dimension_semantics=("arbitrary", "arbitrary", "arbitrary"),
                                             vmem_limit_bytes=VMEM_LIMIT),
        name="latent_attention",
    )(qc, kc, vt)


def _out_kernel(x_ref, g_ref, o1_ref, o2_ref, o3_ref, l1_ref, l2_ref, l3_ref, yb_ref, yc_ref,
                w_o_ref, post_g_ref, out_ref, y_scr):
    l1, l2, l3 = l1_ref[0], l2_ref[0], l3_ref[0]
    big = jnp.maximum(jnp.maximum(l1, l2), l3)
    e1, e2, e3 = jnp.exp2(l1 - big), jnp.exp2(l2 - big), jnp.exp2(l3 - big)
    num = e1 * o1_ref[0].astype(_F32) + e2 * o2_ref[0].astype(_F32) + e3 * o3_ref[0].astype(_F32)
    ya = num / (e1 + e2 + e3)
    y_scr[:, :A_WIDTH] = (ya * g_ref[0, :, :A_WIDTH].astype(_F32)).astype(_BF16)
    y_scr[:, A_WIDTH:A_WIDTH + B_WIDTH] = yb_ref[0] * g_ref[0, :, A_WIDTH:A_WIDTH + B_WIDTH]
    y_scr[:, A_WIDTH + B_WIDTH:] = yc_ref[0] * g_ref[0, :, A_WIDTH + B_WIDTH:]
    y = _dot(y_scr[...], w_o_ref[...])
    out_ref[0] = x_ref[0] + _rms(y, post_g_ref[...])


def _out_proj(x, gates, o_pats, lse_pats, yb, yc, w_o, post_g):
    B, S, _ = x.shape
    row = lambda n: pl.BlockSpec((1, ROW_TILE, n), lambda b, i: (b, i, 0))
    full = lambda a: pl.BlockSpec(a.shape, lambda b, i: (0,) * a.ndim)
    return pl.pallas_call(
        _out_kernel,
        out_shape=jax.ShapeDtypeStruct(x.shape, x.dtype),
        grid=(B, S // ROW_TILE),
        in_specs=[row(D_MODEL), row(D_MIX)] + [row(A_WIDTH)] * 6 + [row(B_WIDTH), row(C_WIDTH),
                                                                     full(w_o), full(post_g)],
        out_specs=row(D_MODEL),
        scratch_shapes=[pltpu.VMEM((ROW_TILE, D_MIX), _BF16)],
        compiler_params=pltpu.CompilerParams(dimension_semantics=("arbitrary", "arbitrary"),
                                             vmem_limit_bytes=VMEM_LIMIT),
        name="out_proj",
    )(x, gates, *o_pats, *lse_pats, yb, yc, w_o, post_g)


def _pair_heads(t, axis):
    parts = jnp.split(t, B_HEADS, axis=axis)
    return jnp.concatenate([parts[h] for h in B_HEAD_ORDER], axis=axis)


def _rot_cols(w):
    half = C_ROPE // 2
    return jnp.concatenate([-w[..., half:], w[..., :half]], axis=-1)


def _layer_weights(w_in, w_uq, w_ukv, w_o):
    o = 0
    cols = {}
    for name, n in (("qa", A_WIDTH), ("ka", A_WIDTH), ("va", A_WIDTH), ("ga", A_WIDTH),
                    ("qb", B_WIDTH), ("kb", B_KV_WIDTH), ("vb", B_KV_WIDTH), ("gb", B_WIDTH),
                    ("cq", Q_LORA), ("ckv", KV_LORA), ("kr", C_ROPE), ("gc", C_WIDTH)):
        cols[name] = w_in[:, o:o + n]
        o += n
    w_a = jnp.concatenate([cols["qa"], cols["ka"], cols["va"]], axis=1)
    w_g = jnp.concatenate([cols["ga"], _pair_heads(cols["gb"], 1), cols["gc"]], axis=1)
    w_b = jnp.concatenate([_pair_heads(cols["qb"], 1), cols["kb"], cols["vb"]], axis=1)
    zeros = lambda n: jnp.zeros((D_MODEL, n), w_in.dtype)
    pad_rope = lambda w: jnp.concatenate([zeros(C_NOPE), w, zeros(C_PAD - C_NOPE - C_ROPE)], axis=1)
    w_c = jnp.concatenate([cols["cq"], cols["ckv"], pad_rope(cols["kr"]), pad_rope(_rot_cols(cols["kr"]))], axis=1)

    uq = w_uq.reshape(Q_LORA, C_HEADS, C_NOPE + C_ROPE)
    zq = lambda n: jnp.zeros((Q_LORA, C_HEADS, n), w_uq.dtype)
    uq_main = jnp.concatenate([uq, zq(C_PAD - C_NOPE - C_ROPE)], axis=-1).reshape(Q_LORA, C_HEADS * C_PAD)
    uq_rot = jnp.concatenate([zq(C_NOPE), _rot_cols(uq[..., C_NOPE:]), zq(C_PAD - C_NOPE - C_ROPE)],
                             axis=-1).reshape(Q_LORA, C_HEADS * C_PAD)
    ukv = w_ukv.reshape(KV_LORA, C_HEADS, C_NOPE + C_V)
    uk = jnp.concatenate([ukv[..., :C_NOPE], jnp.zeros((KV_LORA, C_HEADS, C_PAD - C_NOPE), w_ukv.dtype)],
                         axis=-1).reshape(KV_LORA, C_HEADS * C_PAD)
    uvt = jnp.transpose(ukv[..., C_NOPE:], (1, 2, 0))
    uvt = jnp.concatenate([uvt, jnp.zeros((C_HEADS, C_VROWS - C_V, KV_LORA), w_ukv.dtype)], axis=1)
    uvt = uvt.reshape(C_HEADS * C_VROWS, KV_LORA)

    w_o_p = jnp.concatenate([w_o[:A_WIDTH], _pair_heads(w_o[A_WIDTH:A_WIDTH + B_WIDTH], 0),
                             w_o[A_WIDTH + B_WIDTH:]], axis=0)
    bf = lambda t: t.astype(_BF16)
    return (bf(w_a), bf(w_g), bf(w_b), bf(w_c), bf(uq_main), bf(uq_rot), bf(uk), bf(uvt)), bf(w_o_p)


def kernel(x, positions, pre_norm, w_in, q_a_norm, kv_a_norm, w_uq, w_ukv, sink, w_o, post_norm):
    depth = w_in.shape[0]
    slopes_a, slopes_b = _alibi_slopes()
    cos_t, sin_t = _rope_tables(positions)
    for i in range(depth):
        in_w, w_o_p = _layer_weights(w_in[i], w_uq[i], w_ukv[i], w_o[i])
        qa, ka, va, gates, qb, kb, vb, qc, kc, vt = _in_proj(
            x, cos_t, sin_t, pre_norm[i][None], q_a_norm[i][None], kv_a_norm[i][None], *in_w)
        pats = [_dilated_pattern(qa, ka, va, d, slopes_a) for _, d in A_PATTERNS]
        yb = _windowed(qb, kb, vb, sink[i], slopes_b)
        yc = _latent(qc, kc, vt)
        x = _out_proj(x, gates, [p[0] for p in pats], [p[1] for p in pats], yb, yc, w_o_p, post_norm[i][None])
    return x
```

```python
import functools
import math

import numpy as np
import jax
import jax.numpy as jnp
from jax import lax
from jax.experimental import pallas as pl
from jax.experimental.pallas import tpu as pltpu

D_MODEL = 1024
HEAD_DIM = 64
A_HEADS = 4
A_WIDTH = A_HEADS * HEAD_DIM
A_PATTERNS = ((128, 1), (512, 4), (2048, 16))
A_HALF = 64
B_HEADS = 6
B_KV_HEADS = 2
B_GROUP = B_HEADS // B_KV_HEADS
B_WIDTH = B_HEADS * HEAD_DIM
B_KV_WIDTH = B_KV_HEADS * HEAD_DIM
B_WINDOW = 128
C_HEADS = 6
C_NOPE = 64
C_ROPE = 32
C_V = 64
C_WIDTH = C_HEADS * C_V
Q_LORA = 256
KV_LORA = 128
ROPE_THETA = 10000.0
D_MIX = A_WIDTH + B_WIDTH + C_WIDTH
N_ALIBI = A_HEADS + B_HEADS
RMS_EPS = 1e-6
NEG_INF = -1e30
LOG2E = math.log2(math.e)

LANES = 128
C_PAD = LANES
C_VROWS = 80
C_PAIR = 2

ROW_TILE = 512
A_QBLOCK = 128
A_KWIN = A_QBLOCK + 2 * A_HALF
B_QBLOCK = 256
B_KWIN = B_QBLOCK + 2 * B_WINDOW
C_QBLOCK = 512
C_KBLOCK = 256
VMEM_LIMIT = 48 * 1024 * 1024

B_HEAD_ORDER = (0, 3, 1, 4, 2, 5)

_F32 = jnp.float32
_BF16 = jnp.bfloat16


def _alibi_slopes():
    s = 2.0 ** (-8.0 * np.arange(1, N_ALIBI + 1, dtype=np.float64) / N_ALIBI)
    return [float(v) for v in s[B_HEADS:]], [float(v) for v in s[:B_HEADS]]


def _rms(x, g):
    return x * lax.rsqrt(jnp.mean(x * x, axis=-1, keepdims=True) + RMS_EPS) * g


def _residue_view_shape(batch, seq, d, dtype):
    return jax.ShapeDtypeStruct((batch, seq // d, d * A_WIDTH), dtype)


def _residue_view_rows(d):
    return pl.BlockSpec((1, ROW_TILE // d, d * A_WIDTH), lambda b, i: (b, i, 0))


def _dot(a, b):
    return jnp.dot(a, b, preferred_element_type=_F32)


def _dot_nt(a, b):
    return lax.dot_general(a, b, (((1,), (1,)), ((), ())), preferred_element_type=_F32)


def _rope_table_kernel(pos_ref, freq_ref, cos_ref, sin_ref):
    ang = pos_ref[0].astype(_F32) * freq_ref[...]
    cos_ref[0] = jnp.cos(ang)
    sin_ref[0] = jnp.sin(ang)


def _rope_tables(positions):
    B, S = positions.shape
    half = C_ROPE // 2
    freq = ROPE_THETA ** (-2.0 * jnp.arange(half, dtype=_F32) / C_ROPE)
    lane_freq = jnp.zeros((1, C_PAD), _F32).at[0, C_NOPE:C_NOPE + C_ROPE].set(jnp.concatenate([freq, freq]))
    out = jax.ShapeDtypeStruct((B, S, C_PAD), _F32)
    return pl.pallas_call(
        _rope_table_kernel,
        out_shape=(out, out),
        grid=(B, S // ROW_TILE),
        in_specs=[pl.BlockSpec((1, ROW_TILE, 1), lambda b, i: (b, i, 0)),
                  pl.BlockSpec((1, C_PAD), lambda b, i: (0, 0))],
        out_specs=(pl.BlockSpec((1, ROW_TILE, C_PAD), lambda b, i: (b, i, 0)),
                   pl.BlockSpec((1, ROW_TILE, C_PAD), lambda b, i: (b, i, 0))),
        compiler_params=pltpu.CompilerParams(dimension_semantics=("arbitrary", "arbitrary")),
        name="rope_tables",
    )(positions.reshape(B, S, 1), lane_freq)


def _in_proj_kernel(x_ref, cos_ref, sin_ref, pre_g_ref, qn_g_ref, kvn_g_ref,
                    w_a_ref, w_g_ref, w_b_ref, w_c_ref, w_uq_ref, w_uqr_ref, w_uk_ref, w_uvt_ref,
                    *rest):
    a_refs, (g_ref, qb_ref, kb_ref, vb_ref, qc_ref, kc_ref, vt_ref, a_scr) = rest[:3 * len(A_PATTERNS)], rest[-8:]
    h = _rms(x_ref[0], pre_g_ref[...]).astype(_BF16)
    score_scale = HEAD_DIM ** -0.5 * LOG2E

    pa = _dot(h, w_a_ref[...])
    nchunk = 3 * A_WIDTH // LANES
    for c in range(nchunk):
        chunk = pa[:, c * LANES:(c + 1) * LANES]
        a_scr[c] = chunk * score_scale if c < A_WIDTH // LANES else chunk
    for p, (_, d) in enumerate(A_PATTERNS):
        rows = ROW_TILE // d
        for r in range(d):
            for c in range(nchunk):
                res = a_scr[c, pl.ds(r, rows, stride=d), :] if d > 1 else a_scr[c]
                lane0 = r * A_WIDTH + (c * LANES) % A_WIDTH
                a_refs[3 * p + c * LANES // A_WIDTH][0, :, lane0:lane0 + LANES] = res.astype(_BF16)

    for c in range(0, D_MIX, 256):
        g = _dot(h, w_g_ref[:, c:c + 256])
        g_ref[0, :, c:c + 256] = (g / (1.0 + jnp.exp(-g))).astype(_BF16)

    pb = _dot(h, w_b_ref[...])
    qb_ref[0] = (pb[:, :B_WIDTH] * score_scale).astype(_BF16)
    kb_ref[0] = pb[:, B_WIDTH:B_WIDTH + B_KV_WIDTH].astype(_BF16)
    vb_ref[0] = pb[:, B_WIDTH + B_KV_WIDTH:].astype(_BF16)

    pc = _dot(h, w_c_ref[...])
    cos = cos_ref[0]
    sin = sin_ref[0]
    cq = _rms(pc[:, :Q_LORA], qn_g_ref[...]).astype(_BF16)
    ckv = _rms(pc[:, Q_LORA:Q_LORA + KV_LORA], kvn_g_ref[...]).astype(_BF16)
    k_rope = pc[:, Q_LORA + KV_LORA:Q_LORA + KV_LORA + C_PAD] * cos + pc[:, Q_LORA + KV_LORA + C_PAD:] * sin
    c_scale = (C_NOPE + C_ROPE) ** -0.5 * LOG2E
    for hd in range(C_HEADS):
        cols = slice(hd * C_PAD, (hd + 1) * C_PAD)
        q = _dot(cq, w_uq_ref[:, cols]) * cos + _dot(cq, w_uqr_ref[:, cols]) * sin
        qc_ref[0, hd] = (q * c_scale).astype(_BF16)
        kc_ref[0, hd] = (_dot(ckv, w_uk_ref[:, cols]) + k_rope).astype(_BF16)
    vt = _dot_nt(w_uvt_ref[...], ckv)
    row = lax.broadcasted_iota(jnp.int32, vt.shape, 0)
    ones_row = functools.reduce(jnp.logical_or, [row == hd * C_VROWS + C_V for hd in range(C_HEADS)])
    vt = jnp.where(ones_row, 1.0, vt).astype(_BF16)
    for hd in range(C_HEADS):
        for c in range(ROW_TILE // C_KBLOCK):
            vt_ref[0, hd, c] = vt[hd * C_VROWS:(hd + 1) * C_VROWS, c * C_KBLOCK:(c + 1) * C_KBLOCK]


def _in_proj(x, cos_t, sin_t, pre_g, qn_g, kvn_g, w_a, w_g, w_b, w_c, w_uq, w_uqr, w_uk, w_uvt):
    B, S, _ = x.shape
    nblk = S // ROW_TILE
    row = lambda n: pl.BlockSpec((1, ROW_TILE, n), lambda b, i: (b, i, 0))
    full = lambda a: pl.BlockSpec(a.shape, lambda b, i: (0,) * a.ndim)
    bf = lambda n: jax.ShapeDtypeStruct((B, S, n), _BF16)
    head4 = pl.BlockSpec((1, C_HEADS, ROW_TILE, C_PAD), lambda b, i: (b, 0, i, 0))
    weights = (pre_g, qn_g, kvn_g, w_a, w_g, w_b, w_c, w_uq, w_uqr, w_uk, w_uvt)
    a_shapes = tuple(_residue_view_shape(B, S, d, _BF16) for _, d in A_PATTERNS for _ in range(3))
    a_specs = tuple(_residue_view_rows(d) for _, d in A_PATTERNS for _ in range(3))
    outs = pl.pallas_call(
        _in_proj_kernel,
        out_shape=a_shapes + (bf(D_MIX), bf(B_WIDTH), bf(B_KV_WIDTH), bf(B_KV_WIDTH),
                              jax.ShapeDtypeStruct((B, C_HEADS, S, C_PAD), _BF16),
                              jax.ShapeDtypeStruct((B, C_HEADS, S, C_PAD), _BF16),
                              jax.ShapeDtypeStruct((B, C_HEADS, S // C_KBLOCK, C_VROWS, C_KBLOCK), _BF16)),
        grid=(B, nblk),
        in_specs=[row(D_MODEL), row(C_PAD), row(C_PAD)] + [full(w) for w in weights],
        out_specs=a_specs + (row(D_MIX), row(B_WIDTH), row(B_KV_WIDTH), row(B_KV_WIDTH), head4, head4,
                             pl.BlockSpec((1, C_HEADS, ROW_TILE // C_KBLOCK, C_VROWS, C_KBLOCK),
                                          lambda b, i: (b, 0, i, 0, 0))),
        scratch_shapes=[pltpu.VMEM((3 * A_WIDTH // LANES, ROW_TILE, LANES), _F32)],
        compiler_params=pltpu.CompilerParams(dimension_semantics=("arbitrary", "arbitrary"),
                                             vmem_limit_bytes=VMEM_LIMIT),
        name="in_proj",
    )(x, cos_t, sin_t, *weights)
    n_a = 3 * len(A_PATTERNS)
    return [outs[3 * p:3 * p + 3] for p in range(len(A_PATTERNS))], outs[n_a:]


def _dilated_kernel(q_ref, k_ref, v_ref, o_ref, lse_ref, bias_ref, *, length, dilation, slopes):
    nblk = length // A_QBLOCK
    lane_head = lax.shift_right_logical(lax.broadcasted_iota(jnp.int32, (A_QBLOCK, A_WIDTH), 1),
                                        int(math.log2(HEAD_DIM)))

    @pl.when((pl.program_id(0) == 0) & (pl.program_id(1) == 0))
    def _():
        rel0 = (lax.broadcasted_iota(jnp.int32, (A_QBLOCK, A_KWIN), 1)
                - lax.broadcasted_iota(jnp.int32, (A_QBLOCK, A_KWIN), 0))
        for case, shift in enumerate((0, -A_HALF, -2 * A_HALF)):
            dist = jnp.abs(rel0 + shift)
            for hd in range(A_HEADS):
                bias_ref[case, hd] = jnp.where(dist <= A_HALF,
                                               dist.astype(_F32) * (-slopes[hd] * dilation * LOG2E), NEG_INF)

    def load(blk):
        t0 = pl.multiple_of(blk * A_QBLOCK, A_QBLOCK)
        start = pl.multiple_of(jnp.clip(t0 - A_HALF, 0, length - A_KWIN), A_HALF)
        case = jnp.where(blk == 0, 0, jnp.where(blk == nblk - 1, 2, 1))
        q = q_ref[0, pl.ds(t0, A_QBLOCK), :]
        q_heads = jnp.concatenate([jnp.where(lane_head == hd, q, jnp.zeros_like(q)) for hd in range(A_HEADS)],
                                  axis=0)
        return t0, case, q_heads, k_ref[0, pl.ds(start, A_KWIN), :], v_ref[0, pl.ds(start, A_KWIN), :]

    def softmax(s_all, case):
        ps, ms, ls = [], [], []
        for hd in range(A_HEADS):
            s = s_all[hd * A_QBLOCK:(hd + 1) * A_QBLOCK] + bias_ref[case, hd]
            m = jnp.max(s, axis=-1, keepdims=True)
            p = jnp.exp2(s - m)
            ls.append(jnp.sum(p, axis=-1, keepdims=True))
            ms.append(m)
            ps.append(p.astype(_BF16))
        return jnp.concatenate(ps, axis=0), ms, ls

    def finish(t0, acc_all, ms, ls):
        out = jnp.zeros((A_QBLOCK, A_WIDTH), _F32)
        lse = jnp.zeros((A_QBLOCK, A_WIDTH), _F32)
        for hd in range(A_HEADS):
            mine = lane_head == hd
            out = jnp.where(mine, acc_all[hd * A_QBLOCK:(hd + 1) * A_QBLOCK] * (1.0 / ls[hd]), out)
            lse = jnp.where(mine, ms[hd] + jnp.log2(ls[hd]), lse)
        o_ref[0, pl.ds(t0, A_QBLOCK), :] = out.astype(o_ref.dtype)
        lse_ref[0, pl.ds(t0, A_QBLOCK), :] = lse

    def two_blocks(i, carry):
        blocks = [load(2 * i + j) for j in range(2)]
        scores = [_dot_nt(q_heads, kw) for _, _, q_heads, kw, _ in blocks]
        for (t0, case, _, _, vw), s_all in zip(blocks, scores):
            p_all, ms, ls = softmax(s_all, case)
            finish(t0, _dot(p_all, vw), ms, ls)
        return carry

    lax.fori_loop(0, nblk // 2, two_blocks, 0)


def _dilated_pattern(q, k, v, dilation, slopes):
    B, length, _ = q.shape
    spec = pl.BlockSpec((1, length, A_WIDTH), lambda b, r: (b, 0, r))
    return pl.pallas_call(
        functools.partial(_dilated_kernel, length=length, dilation=dilation, slopes=slopes),
        out_shape=(jax.ShapeDtypeStruct(q.shape, _BF16), jax.ShapeDtypeStruct(q.shape, _F32)),
        grid=(B, dilation),
        in_specs=[spec, spec, spec],
        out_specs=(spec, spec),
        scratch_shapes=[pltpu.VMEM((3, A_HEADS, A_QBLOCK, A_KWIN), _F32)],
        compiler_params=pltpu.CompilerParams(dimension_semantics=("arbitrary", "arbitrary"),
                                             vmem_limit_bytes=VMEM_LIMIT),
        name=f"dilated_d{dilation}",
    )(q, k, v)


def _windowed_kernel(sink_ref, q_ref, k_ref, v_ref, o_ref, bias_ref, *, seq, slopes):
    i = pl.program_id(1)
    last = pl.num_programs(1) - 1
    t0 = pl.multiple_of(i * B_QBLOCK, B_QBLOCK)
    start = pl.multiple_of(jnp.clip(t0 - B_WINDOW, 0, seq - B_KWIN), B_WINDOW)

    @pl.when((i <= 1) | (i == last))
    def _():
        rel0 = (lax.broadcasted_iota(jnp.int32, (B_QBLOCK, B_KWIN), 1)
                - lax.broadcasted_iota(jnp.int32, (B_QBLOCK, B_KWIN), 0))
        dist = jnp.abs(rel0 + (start - t0))
        for idx, head in enumerate(B_HEAD_ORDER):
            bias_ref[idx] = jnp.where(dist <= B_WINDOW, dist.astype(_F32) * (-slopes[head] * LOG2E), NEG_INF)

    kw = k_ref[0, pl.ds(start, B_KWIN), :]
    vw = v_ref[0, pl.ds(start, B_KWIN), :]
    low = lax.broadcasted_iota(jnp.int32, (B_QBLOCK, LANES), 1) < HEAD_DIM
    npair = B_HEADS // 2

    def scores(pair):
        q = q_ref[0, :, pair * LANES:(pair + 1) * LANES]
        zero = jnp.zeros_like(q)
        return _dot_nt(jnp.concatenate([jnp.where(low, q, zero), jnp.where(low, zero, q)], axis=0), kw)

    s_next = scores(0)
    for pair in range(npair):
        s_pair, s_next = s_next, (scores(pair + 1) if pair + 1 < npair else None)
        ps, factors = [], []
        for side in range(2):
            idx = 2 * pair + side
            s = s_pair[side * B_QBLOCK:(side + 1) * B_QBLOCK] + bias_ref[idx]
            m = jnp.max(s, axis=-1, keepdims=True)
            p = jnp.exp2(s - m)
            l = jnp.sum(p, axis=-1, keepdims=True)
            sk = sink_ref[B_HEAD_ORDER[idx]] * LOG2E
            big = jnp.maximum(m, sk)
            e = jnp.exp2(m - big)
            factors.append(e / (l * e + jnp.exp2(sk - big)))
            ps.append(p.astype(_BF16))
        acc = _dot(jnp.concatenate(ps, axis=0), vw)
        out = jnp.where(low, acc[:B_QBLOCK] * factors[0], acc[B_QBLOCK:] * factors[1])
        o_ref[0, :, pair * LANES:(pair + 1) * LANES] = out.astype(o_ref.dtype)


def _windowed(qb, kb, vb, sink, slopes):
    B, S, _ = qb.shape
    kv_spec = pl.BlockSpec((1, S, B_KV_WIDTH), lambda b, i, sink: (b, 0, 0))
    q_spec = pl.BlockSpec((1, B_QBLOCK, B_WIDTH), lambda b, i, sink: (b, i, 0))
    return pl.pallas_call(
        functools.partial(_windowed_kernel, seq=S, slopes=slopes),
        out_shape=jax.ShapeDtypeStruct((B, S, B_WIDTH), _BF16),
        grid_spec=pltpu.PrefetchScalarGridSpec(
            num_scalar_prefetch=1, grid=(B, S // B_QBLOCK),
            in_specs=[q_spec, kv_spec, kv_spec], out_specs=q_spec,
            scratch_shapes=[pltpu.VMEM((B_HEADS, B_QBLOCK, B_KWIN), _F32)]),
        compiler_params=pltpu.CompilerParams(dimension_semantics=("arbitrary", "arbitrary"),
                                             vmem_limit_bytes=VMEM_LIMIT),
        name="windowed_gqa",
    )(sink, qb, kb, vb)


def _latent_kernel(q_ref, k_ref, vt_ref, o_ref, *, seq):
    nkb = seq // C_KBLOCK
    heads = range(C_PAIR)
    qs = [q_ref[0, hd] for hd in heads]

    def scores(hd, j):
        return _dot_nt(k_ref[0, hd, j * C_KBLOCK:(j + 1) * C_KBLOCK, :], qs[hd])

    ms = [jnp.full((1, C_QBLOCK), NEG_INF, _F32) for _ in heads]
    accs = [jnp.zeros((C_VROWS, C_QBLOCK), _F32) for _ in heads]
    s_next = [scores(hd, 0) for hd in heads]
    for j in range(nkb):
        s_cur, s_next = s_next, ([scores(hd, j + 1) for hd in heads] if j + 1 < nkb else None)
        for hd in heads:
            m_new = jnp.maximum(ms[hd], jnp.max(s_cur[hd], axis=0, keepdims=True))
            alpha = jnp.exp2(ms[hd] - m_new)
            p = jnp.exp2(s_cur[hd] - m_new).astype(_BF16)
            accs[hd] = alpha * accs[hd] + _dot(vt_ref[0, hd, j], p)
            ms[hd] = m_new
    halves = []
    for hd in heads:
        acc = jnp.concatenate([accs[hd], jnp.zeros((LANES - C_VROWS, C_QBLOCK), _F32)], axis=0)
        acc_t = acc.T
        halves.append(acc_t[:, :C_V] / acc_t[:, C_V:C_V + 1])
    o_ref[0] = jnp.concatenate(halves, axis=-1).astype(o_ref.dtype)


def _latent(qc, kc, vt):
    B, H, S, _ = qc.shape
    nkb = S // C_KBLOCK
    return pl.pallas_call(
        functools.partial(_latent_kernel, seq=S),
        out_shape=jax.ShapeDtypeStruct((B, S, C_WIDTH), _BF16),
        grid=(B, H // C_PAIR, S // C_QBLOCK),
        in_specs=[pl.BlockSpec((1, C_PAIR, C_QBLOCK, C_PAD), lambda b, g, i: (b, g, i, 0)),
                  pl.BlockSpec((1, C_PAIR, S, C_PAD), lambda b, g, i: (b, g, 0, 0)),
                  pl.BlockSpec((1, C_PAIR, nkb, C_VROWS, C_KBLOCK), lambda b, g, i: (b, g, 0, 0, 0))],
        out_specs=pl.BlockSpec((1, C_QBLOCK, C_PAIR * C_V), lambda b, g, i: (b, i, g)),
        compiler_params=pltpu.CompilerParams(dimension_semantics=("arbitrary", "arbitrary", "arbitrary"),
                                             vmem_limit_bytes=VMEM_LIMIT),
        name="latent_attention",
    )(qc, kc, vt)


def _out_kernel(x_ref, g_ref, o1_ref, o2_ref, o3_ref, l1_ref, l2_ref, l3_ref, yb_ref, yc_ref,
                w_o_ref, post_g_ref, out_ref, y_scr, pat_scr):
    outs, lses = [], []
    for p, ((_, d), o_ref, l_ref) in enumerate(zip(A_PATTERNS, (o1_ref, o2_ref, o3_ref), (l1_ref, l2_ref, l3_ref))):
        if d == 1:
            outs.append(o_ref[0].astype(_F32))
            lses.append(l_ref[0])
            continue
        rows = ROW_TILE // d
        nchunk = A_WIDTH // LANES
        for r in range(d):
            for c in range(nchunk):
                lanes = slice(r * A_WIDTH + c * LANES, r * A_WIDTH + (c + 1) * LANES)
                pat_scr[2 * p, c, pl.ds(r, rows, stride=d), :] = o_ref[0, :, lanes].astype(_F32)
                pat_scr[2 * p + 1, c, pl.ds(r, rows, stride=d), :] = l_ref[0, :, lanes]
        outs.append(jnp.concatenate([pat_scr[2 * p, c] for c in range(nchunk)], axis=-1))
        lses.append(jnp.concatenate([pat_scr[2 * p + 1, c] for c in range(nchunk)], axis=-1))
    l1, l2, l3 = lses
    big = jnp.maximum(jnp.maximum(l1, l2), l3)
    e1, e2, e3 = jnp.exp2(l1 - big), jnp.exp2(l2 - big), jnp.exp2(l3 - big)
    num = e1 * outs[0] + e2 * outs[1] + e3 * outs[2]
    ya = num / (e1 + e2 + e3)
    y_scr[:, :A_WIDTH] = (ya * g_ref[0, :, :A_WIDTH].astype(_F32)).astype(_BF16)
    y_scr[:, A_WIDTH:A_WIDTH + B_WIDTH] = yb_ref[0] * g_ref[0, :, A_WIDTH:A_WIDTH + B_WIDTH]
    y_scr[:, A_WIDTH + B_WIDTH:] = yc_ref[0] * g_ref[0, :, A_WIDTH + B_WIDTH:]
    y = _dot(y_scr[...], w_o_ref[...])
    out_ref[0] = x_ref[0] + _rms(y, post_g_ref[...])


def _out_proj(x, gates, o_pats, lse_pats, yb, yc, w_o, post_g):
    B, S, _ = x.shape
    row = lambda n: pl.BlockSpec((1, ROW_TILE, n), lambda b, i: (b, i, 0))
    full = lambda a: pl.BlockSpec(a.shape, lambda b, i: (0,) * a.ndim)
    return pl.pallas_call(
        _out_kernel,
        out_shape=jax.ShapeDtypeStruct(x.shape, x.dtype),
        grid=(B, S // ROW_TILE),
        in_specs=[row(D_MODEL), row(D_MIX)] + [_residue_view_rows(d) for _, d in A_PATTERNS] * 2
                 + [row(B_WIDTH), row(C_WIDTH), full(w_o), full(post_g)],
        out_specs=row(D_MODEL),
        scratch_shapes=[pltpu.VMEM((ROW_TILE, D_MIX), _BF16),
                        pltpu.VMEM((2 * len(A_PATTERNS), A_WIDTH // LANES, ROW_TILE, LANES), _F32)],
        compiler_params=pltpu.CompilerParams(dimension_semantics=("arbitrary", "arbitrary"),
                                             vmem_limit_bytes=VMEM_LIMIT),
        name="out_proj",
    )(x, gates, *o_pats, *lse_pats, yb, yc, w_o, post_g)


def _pair_heads(t, axis):
    parts = jnp.split(t, B_HEADS, axis=axis)
    return jnp.concatenate([parts[h] for h in B_HEAD_ORDER], axis=axis)


def _rot_cols(w):
    half = C_ROPE // 2
    return jnp.concatenate([-w[..., half:], w[..., :half]], axis=-1)


def _layer_weights(w_in, w_uq, w_ukv, w_o):
    o = 0
    cols = {}
    for name, n in (("qa", A_WIDTH), ("ka", A_WIDTH), ("va", A_WIDTH), ("ga", A_WIDTH),
                    ("qb", B_WIDTH), ("kb", B_KV_WIDTH), ("vb", B_KV_WIDTH), ("gb", B_WIDTH),
                    ("cq", Q_LORA), ("ckv", KV_LORA), ("kr", C_ROPE), ("gc", C_WIDTH)):
        cols[name] = w_in[:, o:o + n]
        o += n
    w_a = jnp.concatenate([cols["qa"], cols["ka"], cols["va"]], axis=1)
    w_g = jnp.concatenate([cols["ga"], _pair_heads(cols["gb"], 1), cols["gc"]], axis=1)
    w_b = jnp.concatenate([_pair_heads(cols["qb"], 1), cols["kb"], cols["vb"]], axis=1)
    zeros = lambda n: jnp.zeros((D_MODEL, n), w_in.dtype)
    pad_rope = lambda w: jnp.concatenate([zeros(C_NOPE), w, zeros(C_PAD - C_NOPE - C_ROPE)], axis=1)
    w_c = jnp.concatenate([cols["cq"], cols["ckv"], pad_rope(cols["kr"]), pad_rope(_rot_cols(cols["kr"]))], axis=1)

    uq = w_uq.reshape(Q_LORA, C_HEADS, C_NOPE + C_ROPE)
    zq = lambda n: jnp.zeros((Q_LORA, C_HEADS, n), w_uq.dtype)
    uq_main = jnp.concatenate([uq, zq(C_PAD - C_NOPE - C_ROPE)], axis=-1).reshape(Q_LORA, C_HEADS * C_PAD)
    uq_rot = jnp.concatenate([zq(C_NOPE), _rot_cols(uq[..., C_NOPE:]), zq(C_PAD - C_NOPE - C_ROPE)],
                             axis=-1).reshape(Q_LORA, C_HEADS * C_PAD)
    ukv = w_ukv.reshape(KV_LORA, C_HEADS, C_NOPE + C_V)
    uk = jnp.concatenate([ukv[..., :C_NOPE], jnp.zeros((KV_LORA, C_HEADS, C_PAD - C_NOPE), w_ukv.dtype)],
                         axis=-1).reshape(KV_LORA, C_HEADS * C_PAD)
    uvt = jnp.transpose(ukv[..., C_NOPE:], (1, 2, 0))
    uvt = jnp.concatenate([uvt, jnp.zeros((C_HEADS, C_VROWS - C_V, KV_LORA), w_ukv.dtype)], axis=1)
    uvt = uvt.reshape(C_HEADS * C_VROWS, KV_LORA)

    w_o_p = jnp.concatenate([w_o[:A_WIDTH], _pair_heads(w_o[A_WIDTH:A_WIDTH + B_WIDTH], 0),
                             w_o[A_WIDTH + B_WIDTH:]], axis=0)
    bf = lambda t: t.astype(_BF16)
    return (bf(w_a), bf(w_g), bf(w_b), bf(w_c), bf(uq_main), bf(uq_rot), bf(uk), bf(uvt)), bf(w_o_p)


def kernel(x, positions, pre_norm, w_in, q_a_norm, kv_a_norm, w_uq, w_ukv, sink, w_o, post_norm):
    depth = w_in.shape[0]
    slopes_a, slopes_b = _alibi_slopes()
    cos_t, sin_t = _rope_tables(positions)
    for i in range(depth):
        in_w, w_o_p = _layer_weights(w_in[i], w_uq[i], w_ukv[i], w_o[i])
        qkv_a, (gates, qb, kb, vb, qc, kc, vt) = _in_proj(
            x, cos_t, sin_t, pre_norm[i][None], q_a_norm[i][None], kv_a_norm[i][None], *in_w)
        pats = [_dilated_pattern(*qkv, d, slopes_a) for qkv, (_, d) in zip(qkv_a, A_PATTERNS)]
        yb = _windowed(qb, kb, vb, sink[i], slopes_b)
        yc = _latent(qc, kc, vt)
        x = _out_proj(x, gates, [p[0] for p in pats], [p[1] for p in pats], yb, yc, w_o_p, post_norm[i][None])
    return x
```

```python
import functools
import math

import numpy as np
import jax
import jax.numpy as jnp
from jax import lax
from jax.experimental import pallas as pl
from jax.experimental.pallas import tpu as pltpu

D_MODEL = 1024
HEAD_DIM = 64
A_HEADS = 4
A_WIDTH = A_HEADS * HEAD_DIM
A_PATTERNS = ((128, 1), (512, 4), (2048, 16))
A_HALF = 64
B_HEADS = 6
B_KV_HEADS = 2
B_GROUP = B_HEADS // B_KV_HEADS
B_WIDTH = B_HEADS * HEAD_DIM
B_KV_WIDTH = B_KV_HEADS * HEAD_DIM
B_WINDOW = 128
C_HEADS = 6
C_NOPE = 64
C_ROPE = 32
C_V = 64
C_WIDTH = C_HEADS * C_V
Q_LORA = 256
KV_LORA = 128
ROPE_THETA = 10000.0
D_MIX = A_WIDTH + B_WIDTH + C_WIDTH
N_ALIBI = A_HEADS + B_HEADS
RMS_EPS = 1e-6
NEG_INF = -1e30
LOG2E = math.log2(math.e)

LANES = 128
C_PAD = LANES
C_VROWS = 80
C_PAIR = 2

ROW_TILE = 512
A_QBLOCK = 128
A_KWIN = A_QBLOCK + 2 * A_HALF
A_RESIDUES_PER_STEP = 4
A_LOOP_UNROLL = 2
B_QBLOCK = 256
B_KWIN = B_QBLOCK + 2 * B_WINDOW
B_QBLOCKS_PER_STEP = 2
B_QSTEP = B_QBLOCKS_PER_STEP * B_QBLOCK
C_QBLOCK = 512
C_KBLOCK = 256
C_QBLOCKS_PER_STEP = 2
C_QSTEP = C_QBLOCKS_PER_STEP * C_QBLOCK
VMEM_LIMIT = 48 * 1024 * 1024

B_HEAD_ORDER = (0, 3, 1, 4, 2, 5)

_F32 = jnp.float32
_BF16 = jnp.bfloat16


def _alibi_slopes():
    s = 2.0 ** (-8.0 * np.arange(1, N_ALIBI + 1, dtype=np.float64) / N_ALIBI)
    return [float(v) for v in s[B_HEADS:]], [float(v) for v in s[:B_HEADS]]


def _rms(x, g):
    return x * lax.rsqrt(jnp.mean(x * x, axis=-1, keepdims=True) + RMS_EPS) * g


def _residue_view_shape(batch, seq, d, dtype):
    return jax.ShapeDtypeStruct((batch, seq // d, d * A_WIDTH), dtype)


def _residue_view_rows(d):
    return pl.BlockSpec((1, ROW_TILE // d, d * A_WIDTH), lambda b, i: (b, i, 0))


def _dot(a, b):
    return jnp.dot(a, b, preferred_element_type=_F32)


def _dot_nt(a, b):
    return lax.dot_general(a, b, (((1,), (1,)), ((), ())), preferred_element_type=_F32)


def _rope_table_kernel(pos_ref, freq_ref, cos_ref, sin_ref):
    ang = pos_ref[0].astype(_F32) * freq_ref[...]
    cos_ref[0] = jnp.cos(ang)
    sin_ref[0] = jnp.sin(ang)


def _rope_tables(positions):
    B, S = positions.shape
    half = C_ROPE // 2
    freq = ROPE_THETA ** (-2.0 * jnp.arange(half, dtype=_F32) / C_ROPE)
    lane_freq = jnp.zeros((1, C_PAD), _F32).at[0, C_NOPE:C_NOPE + C_ROPE].set(jnp.concatenate([freq, freq]))
    out = jax.ShapeDtypeStruct((B, S, C_PAD), _F32)
    return pl.pallas_call(
        _rope_table_kernel,
        out_shape=(out, out),
        grid=(B, S // ROW_TILE),
        in_specs=[pl.BlockSpec((1, ROW_TILE, 1), lambda b, i: (b, i, 0)),
                  pl.BlockSpec((1, C_PAD), lambda b, i: (0, 0))],
        out_specs=(pl.BlockSpec((1, ROW_TILE, C_PAD), lambda b, i: (b, i, 0)),
                   pl.BlockSpec((1, ROW_TILE, C_PAD), lambda b, i: (b, i, 0))),
        compiler_params=pltpu.CompilerParams(dimension_semantics=("arbitrary", "arbitrary")),
        name="rope_tables",
    )(positions.reshape(B, S, 1), lane_freq)


def _in_proj_kernel(x_ref, cos_ref, sin_ref, pre_g_ref, qn_g_ref, kvn_g_ref,
                    w_a_ref, w_g_ref, w_b_ref, w_c_ref, w_uq_ref, w_uqr_ref, w_uk_ref, w_uvt_ref,
                    *rest):
    a_refs, (g_ref, qb_ref, kb_ref, vb_ref, qc_ref, kc_ref, vt_ref, a_scr) = rest[:3 * len(A_PATTERNS)], rest[-8:]
    h = _rms(x_ref[0], pre_g_ref[...]).astype(_BF16)
    score_scale = HEAD_DIM ** -0.5 * LOG2E

    pa = _dot(h, w_a_ref[...])
    nchunk = 3 * A_WIDTH // LANES
    for c in range(nchunk):
        chunk = pa[:, c * LANES:(c + 1) * LANES]
        a_scr[c] = chunk * score_scale if c < A_WIDTH // LANES else chunk
    for p, (_, d) in enumerate(A_PATTERNS):
        rows = ROW_TILE // d
        for r in range(d):
            for c in range(nchunk):
                res = a_scr[c, pl.ds(r, rows, stride=d), :] if d > 1 else a_scr[c]
                lane0 = r * A_WIDTH + (c * LANES) % A_WIDTH
                a_refs[3 * p + c * LANES // A_WIDTH][0, :, lane0:lane0 + LANES] = res.astype(_BF16)

    for c in range(0, D_MIX, 256):
        g = _dot(h, w_g_ref[:, c:c + 256])
        g_ref[0, :, c:c + 256] = (g / (1.0 + jnp.exp(-g))).astype(_BF16)

    pb = _dot(h, w_b_ref[...])
    qb_ref[0] = (pb[:, :B_WIDTH] * score_scale).astype(_BF16)
    kb_ref[0] = pb[:, B_WIDTH:B_WIDTH + B_KV_WIDTH].astype(_BF16)
    vb_ref[0] = pb[:, B_WIDTH + B_KV_WIDTH:].astype(_BF16)

    pc = _dot(h, w_c_ref[...])
    cos = cos_ref[0]
    sin = sin_ref[0]
    cq = _rms(pc[:, :Q_LORA], qn_g_ref[...]).astype(_BF16)
    ckv = _rms(pc[:, Q_LORA:Q_LORA + KV_LORA], kvn_g_ref[...]).astype(_BF16)
    k_rope = pc[:, Q_LORA + KV_LORA:Q_LORA + KV_LORA + C_PAD] * cos + pc[:, Q_LORA + KV_LORA + C_PAD:] * sin
    c_scale = (C_NOPE + C_ROPE) ** -0.5 * LOG2E
    for hd in range(C_HEADS):
        cols = slice(hd * C_PAD, (hd + 1) * C_PAD)
        q = _dot(cq, w_uq_ref[:, cols]) * cos + _dot(cq, w_uqr_ref[:, cols]) * sin
        qc_ref[0, hd] = (q * c_scale).astype(_BF16)
        kc_ref[0, hd] = (_dot(ckv, w_uk_ref[:, cols]) + k_rope).astype(_BF16)
    vt = _dot_nt(w_uvt_ref[...], ckv)
    row = lax.broadcasted_iota(jnp.int32, vt.shape, 0)
    ones_row = functools.reduce(jnp.logical_or, [row == hd * C_VROWS + C_V for hd in range(C_HEADS)])
    vt = jnp.where(ones_row, 1.0, vt).astype(_BF16)
    for hd in range(C_HEADS):
        for c in range(ROW_TILE // C_KBLOCK):
            vt_ref[0, hd, c] = vt[hd * C_VROWS:(hd + 1) * C_VROWS, c * C_KBLOCK:(c + 1) * C_KBLOCK]


def _in_proj(x, cos_t, sin_t, pre_g, qn_g, kvn_g, w_a, w_g, w_b, w_c, w_uq, w_uqr, w_uk, w_uvt):
    B, S, _ = x.shape
    nblk = S // ROW_TILE
    row = lambda n: pl.BlockSpec((1, ROW_TILE, n), lambda b, i: (b, i, 0))
    full = lambda a: pl.BlockSpec(a.shape, lambda b, i: (0,) * a.ndim)
    bf = lambda n: jax.ShapeDtypeStruct((B, S, n), _BF16)
    head4 = pl.BlockSpec((1, C_HEADS, ROW_TILE, C_PAD), lambda b, i: (b, 0, i, 0))
    weights = (pre_g, qn_g, kvn_g, w_a, w_g, w_b, w_c, w_uq, w_uqr, w_uk, w_uvt)
    a_shapes = tuple(_residue_view_shape(B, S, d, _BF16) for _, d in A_PATTERNS for _ in range(3))
    a_specs = tuple(_residue_view_rows(d) for _, d in A_PATTERNS for _ in range(3))
    outs = pl.pallas_call(
        _in_proj_kernel,
        out_shape=a_shapes + (bf(D_MIX), bf(B_WIDTH), bf(B_KV_WIDTH), bf(B_KV_WIDTH),
                              jax.ShapeDtypeStruct((B, C_HEADS, S, C_PAD), _BF16),
                              jax.ShapeDtypeStruct((B, C_HEADS, S, C_PAD), _BF16),
                              jax.ShapeDtypeStruct((B, C_HEADS, S // C_KBLOCK, C_VROWS, C_KBLOCK), _BF16)),
        grid=(B, nblk),
        in_specs=[row(D_MODEL), row(C_PAD), row(C_PAD)] + [full(w) for w in weights],
        out_specs=a_specs + (row(D_MIX), row(B_WIDTH), row(B_KV_WIDTH), row(B_KV_WIDTH), head4, head4,
                             pl.BlockSpec((1, C_HEADS, ROW_TILE // C_KBLOCK, C_VROWS, C_KBLOCK),
                                          lambda b, i: (b, 0, i, 0, 0))),
        scratch_shapes=[pltpu.VMEM((3 * A_WIDTH // LANES, ROW_TILE, LANES), _F32)],
        compiler_params=pltpu.CompilerParams(dimension_semantics=("arbitrary", "arbitrary"),
                                             vmem_limit_bytes=VMEM_LIMIT),
        name="in_proj",
    )(x, cos_t, sin_t, *weights)
    n_a = 3 * len(A_PATTERNS)
    return [outs[3 * p:3 * p + 3] for p in range(len(A_PATTERNS))], outs[n_a:]


def _dilated_kernel(q_ref, k_ref, v_ref, o_ref, lse_ref, bias_ref, *, length, dilation, group, slopes):
    nblk = length // A_QBLOCK
    lane_head = lax.shift_right_logical(lax.broadcasted_iota(jnp.int32, (A_QBLOCK, A_WIDTH), 1),
                                        int(math.log2(HEAD_DIM)))

    @pl.when((pl.program_id(0) == 0) & (pl.program_id(1) == 0))
    def _():
        rel0 = (lax.broadcasted_iota(jnp.int32, (A_QBLOCK, A_KWIN), 1)
                - lax.broadcasted_iota(jnp.int32, (A_QBLOCK, A_KWIN), 0))
        for case, shift in enumerate((0, -A_HALF, -2 * A_HALF)):
            dist = jnp.abs(rel0 + shift)
            for hd in range(A_HEADS):
                bias_ref[case, hd] = jnp.where(dist <= A_HALF,
                                               dist.astype(_F32) * (-slopes[hd] * dilation * LOG2E), NEG_INF)

    def load(blk, lanes):
        t0 = pl.multiple_of(blk * A_QBLOCK, A_QBLOCK)
        start = pl.multiple_of(jnp.clip(t0 - A_HALF, 0, length - A_KWIN), A_HALF)
        case = jnp.where(blk == 0, 0, jnp.where(blk == nblk - 1, 2, 1))
        q = q_ref[0, pl.ds(t0, A_QBLOCK), lanes]
        q_heads = jnp.concatenate([jnp.where(lane_head == hd, q, jnp.zeros_like(q)) for hd in range(A_HEADS)],
                                  axis=0)
        return t0, case, q_heads, k_ref[0, pl.ds(start, A_KWIN), lanes], v_ref[0, pl.ds(start, A_KWIN), lanes]

    def softmax(s_all, case):
        ps, ms, ls = [], [], []
        for hd in range(A_HEADS):
            s = s_all[hd * A_QBLOCK:(hd + 1) * A_QBLOCK] + bias_ref[case, hd]
            m = jnp.max(s, axis=-1, keepdims=True)
            p = jnp.exp2(s - m)
            ls.append(jnp.sum(p, axis=-1, keepdims=True))
            ms.append(m)
            ps.append(p.astype(_BF16))
        return jnp.concatenate(ps, axis=0), ms, ls

    def finish(t0, lanes, acc_all, ms, ls):
        out = jnp.zeros((A_QBLOCK, A_WIDTH), _F32)
        lse = jnp.zeros((A_QBLOCK, A_WIDTH), _F32)
        for hd in range(A_HEADS):
            mine = lane_head == hd
            out = jnp.where(mine, acc_all[hd * A_QBLOCK:(hd + 1) * A_QBLOCK] * (1.0 / ls[hd]), out)
            lse = jnp.where(mine, ms[hd] + jnp.log2(ls[hd]), lse)
        o_ref[0, pl.ds(t0, A_QBLOCK), lanes] = out.astype(o_ref.dtype)
        lse_ref[0, pl.ds(t0, A_QBLOCK), lanes] = lse

    def two_blocks(i, lanes):
        blocks = [load(2 * i + j, lanes) for j in range(2)]
        scores = [_dot_nt(q_heads, kw) for _, _, q_heads, kw, _ in blocks]
        for (t0, case, _, _, vw), s_all in zip(blocks, scores):
            p_all, ms, ls = softmax(s_all, case)
            finish(t0, lanes, _dot(p_all, vw), ms, ls)

    def residue_class(lanes):
        def body(i, carry):
            two_blocks(i, lanes)
            return carry
        lax.fori_loop(0, nblk // 2, body, 0, unroll=min(nblk // 2, A_LOOP_UNROLL))

    for g in range(group):
        residue_class(slice(g * A_WIDTH, (g + 1) * A_WIDTH))


def _dilated_pattern(q, k, v, dilation, slopes):
    B, length, _ = q.shape
    group = min(dilation, A_RESIDUES_PER_STEP)
    spec = pl.BlockSpec((1, length, group * A_WIDTH), lambda b, r: (b, 0, r))
    return pl.pallas_call(
        functools.partial(_dilated_kernel, length=length, dilation=dilation, group=group, slopes=slopes),
        out_shape=(jax.ShapeDtypeStruct(q.shape, _BF16), jax.ShapeDtypeStruct(q.shape, _F32)),
        grid=(B, dilation // group),
        in_specs=[spec, spec, spec],
        out_specs=(spec, spec),
        scratch_shapes=[pltpu.VMEM((3, A_HEADS, A_QBLOCK, A_KWIN), _F32)],
        compiler_params=pltpu.CompilerParams(dimension_semantics=("arbitrary", "arbitrary"),
                                             vmem_limit_bytes=VMEM_LIMIT),
        name=f"dilated_d{dilation}",
    )(q, k, v)


def _windowed_kernel(sink_ref, q_ref, k_ref, v_ref, o_ref, bias_ref, *, seq, slopes):
    nblk = seq // B_QBLOCK

    @pl.when((pl.program_id(0) == 0) & (pl.program_id(1) == 0))
    def _():
        rel0 = (lax.broadcasted_iota(jnp.int32, (B_QBLOCK, B_KWIN), 1)
                - lax.broadcasted_iota(jnp.int32, (B_QBLOCK, B_KWIN), 0))
        for case, shift in enumerate((0, -B_WINDOW, -2 * B_WINDOW)):
            dist = jnp.abs(rel0 + shift)
            for idx, head in enumerate(B_HEAD_ORDER):
                bias_ref[case, idx] = jnp.where(dist <= B_WINDOW, dist.astype(_F32) * (-slopes[head] * LOG2E),
                                                NEG_INF)

    low = lax.broadcasted_iota(jnp.int32, (B_QBLOCK, LANES), 1) < HEAD_DIM
    npair = B_HEADS // 2
    blocks = []
    for j in range(B_QBLOCKS_PER_STEP):
        blk = pl.program_id(1) * B_QBLOCKS_PER_STEP + j
        t0 = pl.multiple_of(blk * B_QBLOCK, B_QBLOCK)
        start = pl.multiple_of(jnp.clip(t0 - B_WINDOW, 0, seq - B_KWIN), B_WINDOW)
        case = jnp.where(blk == 0, 0, jnp.where(blk == nblk - 1, 2, 1))
        rows = slice(j * B_QBLOCK, (j + 1) * B_QBLOCK)
        blocks.append((rows, case, k_ref[0, pl.ds(start, B_KWIN), :], v_ref[0, pl.ds(start, B_KWIN), :]))

    def scores(item):
        (rows, _, kw, _), pair = item
        q = q_ref[0, rows, pair * LANES:(pair + 1) * LANES]
        zero = jnp.zeros_like(q)
        return _dot_nt(jnp.concatenate([jnp.where(low, q, zero), jnp.where(low, zero, q)], axis=0), kw)

    items = [(block, pair) for block in blocks for pair in range(npair)]
    s_next = scores(items[0])
    for n, ((rows, case, _, vw), pair) in enumerate(items):
        s_pair, s_next = s_next, (scores(items[n + 1]) if n + 1 < len(items) else None)
        ps, factors = [], []
        for side in range(2):
            idx = 2 * pair + side
            s = s_pair[side * B_QBLOCK:(side + 1) * B_QBLOCK] + bias_ref[case, idx]
            m = jnp.max(s, axis=-1, keepdims=True)
            p = jnp.exp2(s - m)
            l = jnp.sum(p, axis=-1, keepdims=True)
            sk = sink_ref[B_HEAD_ORDER[idx]] * LOG2E
            big = jnp.maximum(m, sk)
            e = jnp.exp2(m - big)
            factors.append(e / (l * e + jnp.exp2(sk - big)))
            ps.append(p.astype(_BF16))
        acc = _dot(jnp.concatenate(ps, axis=0), vw)
        out = jnp.where(low, acc[:B_QBLOCK] * factors[0], acc[B_QBLOCK:] * factors[1])
        o_ref[0, rows, pair * LANES:(pair + 1) * LANES] = out.astype(o_ref.dtype)


def _windowed(qb, kb, vb, sink, slopes):
    B, S, _ = qb.shape
    kv_spec = pl.BlockSpec((1, S, B_KV_WIDTH), lambda b, i, sink: (b, 0, 0))
    q_spec = pl.BlockSpec((1, B_QSTEP, B_WIDTH), lambda b, i, sink: (b, i, 0))
    return pl.pallas_call(
        functools.partial(_windowed_kernel, seq=S, slopes=slopes),
        out_shape=jax.ShapeDtypeStruct((B, S, B_WIDTH), _BF16),
        grid_spec=pltpu.PrefetchScalarGridSpec(
            num_scalar_prefetch=1, grid=(B, S // B_QSTEP),
            in_specs=[q_spec, kv_spec, kv_spec], out_specs=q_spec,
            scratch_shapes=[pltpu.VMEM((3, B_HEADS, B_QBLOCK, B_KWIN), _F32)]),
        compiler_params=pltpu.CompilerParams(dimension_semantics=("arbitrary", "arbitrary"),
                                             vmem_limit_bytes=VMEM_LIMIT),
        name="windowed_gqa",
    )(sink, qb, kb, vb)


def _latent_kernel(q_ref, k_ref, vt_ref, o_ref, *, seq):
    nkb = seq // C_KBLOCK
    heads = range(C_PAIR)

    for qb in range(C_QBLOCKS_PER_STEP):
        rows = slice(qb * C_QBLOCK, (qb + 1) * C_QBLOCK)
        qs = [q_ref[0, hd, rows, :] for hd in heads]

        def scores(hd, j):
            return _dot_nt(k_ref[0, hd, j * C_KBLOCK:(j + 1) * C_KBLOCK, :], qs[hd])

        ms = [jnp.full((1, C_QBLOCK), NEG_INF, _F32) for _ in heads]
        accs = [jnp.zeros((C_VROWS, C_QBLOCK), _F32) for _ in heads]
        s_next = [scores(hd, 0) for hd in heads]
        for j in range(nkb):
            s_cur, s_next = s_next, ([scores(hd, j + 1) for hd in heads] if j + 1 < nkb else None)
            for hd in heads:
                m_new = jnp.maximum(ms[hd], jnp.max(s_cur[hd], axis=0, keepdims=True))
                alpha = jnp.exp2(ms[hd] - m_new)
                p = jnp.exp2(s_cur[hd] - m_new).astype(_BF16)
                accs[hd] = alpha * accs[hd] + _dot(vt_ref[0, hd, j], p)
                ms[hd] = m_new
        out_t = jnp.concatenate([accs[hd][:C_V] * (1.0 / accs[hd][C_V:C_V + 1]) for hd in heads], axis=0)
        o_ref[0, rows, :] = out_t.T.astype(o_ref.dtype)


def _latent(qc, kc, vt):
    B, H, S, _ = qc.shape
    nkb = S // C_KBLOCK
    return pl.pallas_call(
        functools.partial(_latent_kernel, seq=S),
        out_shape=jax.ShapeDtypeStruct((B, S, C_WIDTH), _BF16),
        grid=(B, H // C_PAIR, S // C_QSTEP),
        in_specs=[pl.BlockSpec((1, C_PAIR, C_QSTEP, C_PAD), lambda b, g, i: (b, g, i, 0)),
                  pl.BlockSpec((1, C_PAIR, S, C_PAD), lambda b, g, i: (b, g, 0, 0)),
                  pl.BlockSpec((1, C_PAIR, nkb, C_VROWS, C_KBLOCK), lambda b, g, i: (b, g, 0, 0, 0))],
        out_specs=pl.BlockSpec((1, C_QSTEP, C_PAIR * C_V), lambda b, g, i: (b, i, g)),
        compiler_params=pltpu.CompilerParams(dimension_semantics=("arbitrary", "arbitrary", "arbitrary"),
                                             vmem_limit_bytes=VMEM_LIMIT),
        name="latent_attention",
    )(qc, kc, vt)


def _out_kernel(x_ref, g_ref, o1_ref, o2_ref, o3_ref, l1_ref, l2_ref, l3_ref, yb_ref, yc_ref,
                w_o_ref, post_g_ref, out_ref, y_scr, pat_scr):
    outs, lses = [], []
    for p, ((_, d), o_ref, l_ref) in enumerate(zip(A_PATTERNS, (o1_ref, o2_ref, o3_ref), (l1_ref, l2_ref, l3_ref))):
        if d == 1:
            outs.append(o_ref[0].astype(_F32))
            lses.append(l_ref[0])
            continue
        rows = ROW_TILE // d
        nchunk = A_WIDTH // LANES
        for r in range(d):
            for c in range(nchunk):
                lanes = slice(r * A_WIDTH + c * LANES, r * A_WIDTH + (c + 1) * LANES)
                pat_scr[2 * p, c, pl.ds(r, rows, stride=d), :] = o_ref[0, :, lanes].astype(_F32)
                pat_scr[2 * p + 1, c, pl.ds(r, rows, stride=d), :] = l_ref[0, :, lanes]
        outs.append(jnp.concatenate([pat_scr[2 * p, c] for c in range(nchunk)], axis=-1))
        lses.append(jnp.concatenate([pat_scr[2 * p + 1, c] for c in range(nchunk)], axis=-1))
    l1, l2, l3 = lses
    big = jnp.maximum(jnp.maximum(l1, l2), l3)
    e1, e2, e3 = jnp.exp2(l1 - big), jnp.exp2(l2 - big), jnp.exp2(l3 - big)
    num = e1 * outs[0] + e2 * outs[1] + e3 * outs[2]
    ya = num / (e1 + e2 + e3)
    y_scr[:, :A_WIDTH] = (ya * g_ref[0, :, :A_WIDTH].astype(_F32)).astype(_BF16)
    y_scr[:, A_WIDTH:A_WIDTH + B_WIDTH] = yb_ref[0] * g_ref[0, :, A_WIDTH:A_WIDTH + B_WIDTH]
    y_scr[:, A_WIDTH + B_WIDTH:] = yc_ref[0] * g_ref[0, :, A_WIDTH + B_WIDTH:]
    y = _dot(y_scr[...], w_o_ref[...])
    out_ref[0] = x_ref[0] + _rms(y, post_g_ref[...])


def _out_proj(x, gates, o_pats, lse_pats, yb, yc, w_o, post_g):
    B, S, _ = x.shape
    row = lambda n: pl.BlockSpec((1, ROW_TILE, n), lambda b, i: (b, i, 0))
    full = lambda a: pl.BlockSpec(a.shape, lambda b, i: (0,) * a.ndim)
    return pl.pallas_call(
        _out_kernel,
        out_shape=jax.ShapeDtypeStruct(x.shape, x.dtype),
        grid=(B, S // ROW_TILE),
        in_specs=[row(D_MODEL), row(D_MIX)] + [_residue_view_rows(d) for _, d in A_PATTERNS] * 2
                 + [row(B_WIDTH), row(C_WIDTH), full(w_o), full(post_g)],
        out_specs=row(D_MODEL),
        scratch_shapes=[pltpu.VMEM((ROW_TILE, D_MIX), _BF16),
                        pltpu.VMEM((2 * len(A_PATTERNS), A_WIDTH // LANES, ROW_TILE, LANES), _F32)],
        compiler_params=pltpu.CompilerParams(dimension_semantics=("arbitrary", "arbitrary"),
                                             vmem_limit_bytes=VMEM_LIMIT),
        name="out_proj",
    )(x, gates, *o_pats, *lse_pats, yb, yc, w_o, post_g)


def _pair_heads(t, axis):
    parts = jnp.split(t, B_HEADS, axis=axis)
    return jnp.concatenate([parts[h] for h in B_HEAD_ORDER], axis=axis)


def _rot_cols(w):
    half = C_ROPE // 2
    return jnp.concatenate([-w[..., half:], w[..., :half]], axis=-1)


def _layer_weights(w_in, w_uq, w_ukv, w_o):
    o = 0
    cols = {}
    for name, n in (("qa", A_WIDTH), ("ka", A_WIDTH), ("va", A_WIDTH), ("ga", A_WIDTH),
                    ("qb", B_WIDTH), ("kb", B_KV_WIDTH), ("vb", B_KV_WIDTH), ("gb", B_WIDTH),
                    ("cq", Q_LORA), ("ckv", KV_LORA), ("kr", C_ROPE), ("gc", C_WIDTH)):
        cols[name] = w_in[:, o:o + n]
        o += n
    w_a = jnp.concatenate([cols["qa"], cols["ka"], cols["va"]], axis=1)
    w_g = jnp.concatenate([cols["ga"], _pair_heads(cols["gb"], 1), cols["gc"]], axis=1)
    w_b = jnp.concatenate([_pair_heads(cols["qb"], 1), cols["kb"], cols["vb"]], axis=1)
    zeros = lambda n: jnp.zeros((D_MODEL, n), w_in.dtype)
    pad_rope = lambda w: jnp.concatenate([zeros(C_NOPE), w, zeros(C_PAD - C_NOPE - C_ROPE)], axis=1)
    w_c = jnp.concatenate([cols["cq"], cols["ckv"], pad_rope(cols["kr"]), pad_rope(_rot_cols(cols["kr"]))], axis=1)

    uq = w_uq.reshape(Q_LORA, C_HEADS, C_NOPE + C_ROPE)
    zq = lambda n: jnp.zeros((Q_LORA, C_HEADS, n), w_uq.dtype)
    uq_main = jnp.concatenate([uq, zq(C_PAD - C_NOPE - C_ROPE)], axis=-1).reshape(Q_LORA, C_HEADS * C_PAD)
    uq_rot = jnp.concatenate([zq(C_NOPE), _rot_cols(uq[..., C_NOPE:]), zq(C_PAD - C_NOPE - C_ROPE)],
                             axis=-1).reshape(Q_LORA, C_HEADS * C_PAD)
    ukv = w_ukv.reshape(KV_LORA, C_HEADS, C_NOPE + C_V)
    uk = jnp.concatenate([ukv[..., :C_NOPE], jnp.zeros((KV_LORA, C_HEADS, C_PAD - C_NOPE), w_ukv.dtype)],
                         axis=-1).reshape(KV_LORA, C_HEADS * C_PAD)
    uvt = jnp.transpose(ukv[..., C_NOPE:], (1, 2, 0))
    uvt = jnp.concatenate([uvt, jnp.zeros((C_HEADS, C_VROWS - C_V, KV_LORA), w_ukv.dtype)], axis=1)
    uvt = uvt.reshape(C_HEADS * C_VROWS, KV_LORA)

    w_o_p = jnp.concatenate([w_o[:A_WIDTH], _pair_heads(w_o[A_WIDTH:A_WIDTH + B_WIDTH], 0),
                             w_o[A_WIDTH + B_WIDTH:]], axis=0)
    bf = lambda t: t.astype(_BF16)
    return (bf(w_a), bf(w_g), bf(w_b), bf(w_c), bf(uq_main), bf(uq_rot), bf(uk), bf(uvt)), bf(w_o_p)


def kernel(x, positions, pre_norm, w_in, q_a_norm, kv_a_norm, w_uq, w_ukv, sink, w_o, post_norm):
    depth = w_in.shape[0]
    slopes_a, slopes_b = _alibi_slopes()
    cos_t, sin_t = _rope_tables(positions)
    for i in range(depth):
        in_w, w_o_p = _layer_weights(w_in[i], w_uq[i], w_ukv[i], w_o[i])
        qkv_a, (gates, qb, kb, vb, qc, kc, vt) = _in_proj(
            x, cos_t, sin_t, pre_norm[i][None], q_a_norm[i][None], kv_a_norm[i][None], *in_w)
        pats = [_dilated_pattern(*qkv, d, slopes_a) for qkv, (_, d) in zip(qkv_a, A_PATTERNS)]
        yb = _windowed(qb, kb, vb, sink[i], slopes_b)
        yc = _latent(qc, kc, vt)
        x = _out_proj(x, gates, [p[0] for p in pats], [p[1] for p in pats], yb, yc, w_o_p, post_norm[i][None])
    return x
```

```python
import functools
import math

import numpy as np
import jax
import jax.numpy as jnp
from jax import lax
from jax.experimental import pallas as pl
from jax.experimental.pallas import tpu as pltpu

D_MODEL = 1024
HEAD_DIM = 64
A_HEADS = 4
A_WIDTH = A_HEADS * HEAD_DIM
A_PATTERNS = ((128, 1), (512, 4), (2048, 16))
A_HALF = 64
B_HEADS = 6
B_KV_HEADS = 2
B_GROUP = B_HEADS // B_KV_HEADS
B_WIDTH = B_HEADS * HEAD_DIM
B_KV_WIDTH = B_KV_HEADS * HEAD_DIM
B_WINDOW = 128
C_HEADS = 6
C_NOPE = 64
C_ROPE = 32
C_V = 64
C_WIDTH = C_HEADS * C_V
Q_LORA = 256
KV_LORA = 128
ROPE_THETA = 10000.0
D_MIX = A_WIDTH + B_WIDTH + C_WIDTH
N_ALIBI = A_HEADS + B_HEADS
RMS_EPS = 1e-6
NEG_INF = -1e30
LOG2E = math.log2(math.e)

LANES = 128
C_PAD = LANES
C_VROWS = 80
C_PAIR = 2

ROW_TILE = 512
A_QBLOCK = 128
A_KWIN = A_QBLOCK + 2 * A_HALF
A_RESIDUES_PER_STEP = 4
A_LOOP_UNROLL = 2
B_QBLOCK = 128
B_KWIN = B_QBLOCK + 2 * B_WINDOW
B_QBLOCKS_PER_STEP = 4
B_QSTEP = B_QBLOCKS_PER_STEP * B_QBLOCK
C_QBLOCK = 512
C_KBLOCK = 256
C_QBLOCKS_PER_STEP = 2
C_QSTEP = C_QBLOCKS_PER_STEP * C_QBLOCK
VMEM_LIMIT = 48 * 1024 * 1024

B_HEAD_ORDER = (0, 3, 1, 4, 2, 5)

_F32 = jnp.float32
_BF16 = jnp.bfloat16


def _alibi_slopes():
    s = 2.0 ** (-8.0 * np.arange(1, N_ALIBI + 1, dtype=np.float64) / N_ALIBI)
    return [float(v) for v in s[B_HEADS:]], [float(v) for v in s[:B_HEADS]]


def _rms(x, g):
    return x * lax.rsqrt(jnp.mean(x * x, axis=-1, keepdims=True) + RMS_EPS) * g


def _residue_view_shape(batch, seq, d, dtype):
    return jax.ShapeDtypeStruct((batch, seq // d, d * A_WIDTH), dtype)


def _residue_view_rows(d):
    return pl.BlockSpec((1, ROW_TILE // d, d * A_WIDTH), lambda b, i: (b, i, 0))


def _dot(a, b):
    return jnp.dot(a, b, preferred_element_type=_F32)


def _dot_nt(a, b):
    return lax.dot_general(a, b, (((1,), (1,)), ((), ())), preferred_element_type=_F32)


def _rope_table_kernel(pos_ref, freq_ref, cos_ref, sin_ref):
    ang = freq_ref[...] * pos_ref[0].astype(_F32)
    cos, sin = jnp.cos(ang), jnp.sin(ang)
    tokens = ang.shape[1]
    ones = jnp.ones((C_NOPE, tokens), _F32)
    zeros = jnp.zeros((C_NOPE, tokens), _F32)
    tail = zeros[:C_PAD - C_NOPE - C_ROPE]
    cos_ref[0] = jnp.concatenate([ones, cos, cos, tail], axis=0).T
    sin_ref[0] = jnp.concatenate([zeros, sin, sin, tail], axis=0).T


def _rope_tables(positions):
    B, S = positions.shape
    half = C_ROPE // 2
    freq = ROPE_THETA ** (-2.0 * jnp.arange(half, dtype=_F32) / C_ROPE)
    out = jax.ShapeDtypeStruct((B, S, C_PAD), _F32)
    return pl.pallas_call(
        _rope_table_kernel,
        out_shape=(out, out),
        grid=(B, S // ROW_TILE),
        in_specs=[pl.BlockSpec((1, 1, ROW_TILE), lambda b, i: (b, 0, i)),
                  pl.BlockSpec((half, 1), lambda b, i: (0, 0))],
        out_specs=(pl.BlockSpec((1, ROW_TILE, C_PAD), lambda b, i: (b, i, 0)),
                   pl.BlockSpec((1, ROW_TILE, C_PAD), lambda b, i: (b, i, 0))),
        compiler_params=pltpu.CompilerParams(dimension_semantics=("arbitrary", "arbitrary")),
        name="rope_tables",
    )(positions.reshape(B, 1, S), freq.reshape(half, 1))


def _in_proj_kernel(x_ref, cos_ref, sin_ref, pre_g_ref, qn_g_ref, kvn_g_ref,
                    w_a_ref, w_g_ref, w_b_ref, w_c_ref, w_uq_ref, w_uk_ref, w_uvt_ref,
                    *rest):
    a_refs, (g_ref, qb_ref, kb_ref, vb_ref, qc_ref, kc_ref, vt_ref, a_scr) = rest[:3 * len(A_PATTERNS)], rest[-8:]
    h = _rms(x_ref[0], pre_g_ref[...]).astype(_BF16)
    score_scale = HEAD_DIM ** -0.5 * LOG2E

    pa = _dot(h, w_a_ref[...])
    nchunk = 3 * A_WIDTH // LANES
    for c in range(nchunk):
        chunk = pa[:, c * LANES:(c + 1) * LANES]
        a_scr[c] = chunk * score_scale if c < A_WIDTH // LANES else chunk
    for p, (_, d) in enumerate(A_PATTERNS):
        rows = ROW_TILE // d
        for r in range(d):
            for c in range(nchunk):
                res = a_scr[c, pl.ds(r, rows, stride=d), :] if d > 1 else a_scr[c]
                lane0 = r * A_WIDTH + (c * LANES) % A_WIDTH
                a_refs[3 * p + c * LANES // A_WIDTH][0, :, lane0:lane0 + LANES] = res.astype(_BF16)

    for c in range(0, D_MIX, 256):
        g = _dot(h, w_g_ref[:, c:c + 256])
        g_ref[0, :, c:c + 256] = (g / (1.0 + jnp.exp(-g))).astype(_BF16)

    pb = _dot(h, w_b_ref[...])
    qb_ref[0] = (pb[:, :B_WIDTH] * score_scale).astype(_BF16)
    kb_ref[0] = pb[:, B_WIDTH:B_WIDTH + B_KV_WIDTH].astype(_BF16)
    vb_ref[0] = pb[:, B_WIDTH + B_KV_WIDTH:].astype(_BF16)

    pc = _dot(h, w_c_ref[...])
    cos = cos_ref[0]
    sin = sin_ref[0]

    def rotary(t):
        return t * cos + pltpu.roll(t, C_PAD - C_ROPE, axis=1) * sin

    cq = _rms(pc[:, :Q_LORA], qn_g_ref[...]).astype(_BF16)
    ckv = _rms(pc[:, Q_LORA:Q_LORA + KV_LORA], kvn_g_ref[...]).astype(_BF16)
    k_rope = rotary(pc[:, Q_LORA + KV_LORA:])
    c_scale = (C_NOPE + C_ROPE) ** -0.5 * LOG2E
    for hd in range(C_HEADS):
        cols = slice(hd * C_PAD, (hd + 1) * C_PAD)
        qc_ref[0, hd] = (rotary(_dot(cq, w_uq_ref[:, cols])) * c_scale).astype(_BF16)
        kc_ref[0, hd] = (_dot(ckv, w_uk_ref[:, cols]) + k_rope).astype(_BF16)
    vt = _dot_nt(w_uvt_ref[...], ckv)
    row = lax.broadcasted_iota(jnp.int32, vt.shape, 0)
    ones_row = functools.reduce(jnp.logical_or, [row == hd * C_VROWS + C_V for hd in range(C_HEADS)])
    vt = jnp.where(ones_row, 1.0, vt).astype(_BF16)
    for hd in range(C_HEADS):
        for c in range(ROW_TILE // C_KBLOCK):
            vt_ref[0, hd, c] = vt[hd * C_VROWS:(hd + 1) * C_VROWS, c * C_KBLOCK:(c + 1) * C_KBLOCK]


def _in_proj(x, cos_t, sin_t, pre_g, qn_g, kvn_g, w_a, w_g, w_b, w_c, w_uq, w_uk, w_uvt):
    B, S, _ = x.shape
    nblk = S // ROW_TILE
    row = lambda n: pl.BlockSpec((1, ROW_TILE, n), lambda b, i: (b, i, 0))
    full = lambda a: pl.BlockSpec(a.shape, lambda b, i: (0,) * a.ndim)
    bf = lambda n: jax.ShapeDtypeStruct((B, S, n), _BF16)
    head4 = pl.BlockSpec((1, C_HEADS, ROW_TILE, C_PAD), lambda b, i: (b, 0, i, 0))
    weights = (pre_g, qn_g, kvn_g, w_a, w_g, w_b, w_c, w_uq, w_uk, w_uvt)
    a_shapes = tuple(_residue_view_shape(B, S, d, _BF16) for _, d in A_PATTERNS for _ in range(3))
    a_specs = tuple(_residue_view_rows(d) for _, d in A_PATTERNS for _ in range(3))
    outs = pl.pallas_call(
        _in_proj_kernel,
        out_shape=a_shapes + (bf(D_MIX), bf(B_WIDTH), bf(B_KV_WIDTH), bf(B_KV_WIDTH),
                              jax.ShapeDtypeStruct((B, C_HEADS, S, C_PAD), _BF16),
                              jax.ShapeDtypeStruct((B, C_HEADS, S, C_PAD), _BF16),
                              jax.ShapeDtypeStruct((B, C_HEADS, S // C_KBLOCK, C_VROWS, C_KBLOCK), _BF16)),
        grid=(B, nblk),
        in_specs=[row(D_MODEL), row(C_PAD), row(C_PAD)] + [full(w) for w in weights],
        out_specs=a_specs + (row(D_MIX), row(B_WIDTH), row(B_KV_WIDTH), row(B_KV_WIDTH), head4, head4,
                             pl.BlockSpec((1, C_HEADS, ROW_TILE // C_KBLOCK, C_VROWS, C_KBLOCK),
                                          lambda b, i: (b, 0, i, 0, 0))),
        scratch_shapes=[pltpu.VMEM((3 * A_WIDTH // LANES, ROW_TILE, LANES), _F32)],
        compiler_params=pltpu.CompilerParams(dimension_semantics=("arbitrary", "arbitrary"),
                                             vmem_limit_bytes=VMEM_LIMIT),
        name="in_proj",
    )(x, cos_t, sin_t, *weights)
    n_a = 3 * len(A_PATTERNS)
    return [outs[3 * p:3 * p + 3] for p in range(len(A_PATTERNS))], outs[n_a:]


def _dilated_kernel(q_ref, k_ref, v_ref, o_ref, lse_ref, bias_ref, *, length, dilation, group, slopes):
    nblk = length // A_QBLOCK
    lane_head = lax.shift_right_logical(lax.broadcasted_iota(jnp.int32, (A_QBLOCK, A_WIDTH), 1),
                                        int(math.log2(HEAD_DIM)))

    @pl.when((pl.program_id(0) == 0) & (pl.program_id(1) == 0))
    def _():
        rel0 = (lax.broadcasted_iota(jnp.int32, (A_QBLOCK, A_KWIN), 1)
                - lax.broadcasted_iota(jnp.int32, (A_QBLOCK, A_KWIN), 0))
        for case, shift in enumerate((0, -A_HALF, -2 * A_HALF)):
            dist = jnp.abs(rel0 + shift)
            for hd in range(A_HEADS):
                bias_ref[case, hd] = jnp.where(dist <= A_HALF,
                                               dist.astype(_F32) * (-slopes[hd] * dilation * LOG2E), NEG_INF)

    def load(blk, lanes):
        t0 = pl.multiple_of(blk * A_QBLOCK, A_QBLOCK)
        start = pl.multiple_of(jnp.clip(t0 - A_HALF, 0, length - A_KWIN), A_HALF)
        case = jnp.where(blk == 0, 0, jnp.where(blk == nblk - 1, 2, 1))
        q = q_ref[0, pl.ds(t0, A_QBLOCK), lanes]
        q_heads = jnp.concatenate([jnp.where(lane_head == hd, q, jnp.zeros_like(q)) for hd in range(A_HEADS)],
                                  axis=0)
        return t0, case, q_heads, k_ref[0, pl.ds(start, A_KWIN), lanes], v_ref[0, pl.ds(start, A_KWIN), lanes]

    def softmax(s_all, case):
        ps, ms, ls = [], [], []
        for hd in range(A_HEADS):
            s = s_all[hd * A_QBLOCK:(hd + 1) * A_QBLOCK] + bias_ref[case, hd]
            m = jnp.max(s, axis=-1, keepdims=True)
            p = jnp.exp2(s - m)
            ls.append(jnp.sum(p, axis=-1, keepdims=True))
            ms.append(m)
            ps.append(p.astype(_BF16))
        return jnp.concatenate(ps, axis=0), ms, ls

    def finish(t0, lanes, acc_all, ms, ls):
        out = jnp.zeros((A_QBLOCK, A_WIDTH), _F32)
        lse = jnp.zeros((A_QBLOCK, A_WIDTH), _F32)
        for hd in range(A_HEADS):
            mine = lane_head == hd
            out = jnp.where(mine, acc_all[hd * A_QBLOCK:(hd + 1) * A_QBLOCK] * (1.0 / ls[hd]), out)
            lse = jnp.where(mine, ms[hd] + jnp.log2(ls[hd]), lse)
        o_ref[0, pl.ds(t0, A_QBLOCK), lanes] = out.astype(o_ref.dtype)
        lse_ref[0, pl.ds(t0, A_QBLOCK), lanes] = lse

    def two_blocks(i, lanes):
        blocks = [load(2 * i + j, lanes) for j in range(2)]
        scores = [_dot_nt(q_heads, kw) for _, _, q_heads, kw, _ in blocks]
        for (t0, case, _, _, vw), s_all in zip(blocks, scores):
            p_all, ms, ls = softmax(s_all, case)
            finish(t0, lanes, _dot(p_all, vw), ms, ls)

    def residue_class(lanes):
        def body(i, carry):
            two_blocks(i, lanes)
            return carry
        lax.fori_loop(0, nblk // 2, body, 0, unroll=min(nblk // 2, A_LOOP_UNROLL))

    for g in range(group):
        residue_class(slice(g * A_WIDTH, (g + 1) * A_WIDTH))


def _dilated_pattern(q, k, v, dilation, slopes):
    B, length, _ = q.shape
    group = min(dilation, A_RESIDUES_PER_STEP)
    spec = pl.BlockSpec((1, length, group * A_WIDTH), lambda b, r: (b, 0, r))
    return pl.pallas_call(
        functools.partial(_dilated_kernel, length=length, dilation=dilation, group=group, slopes=slopes),
        out_shape=(jax.ShapeDtypeStruct(q.shape, _BF16), jax.ShapeDtypeStruct(q.shape, _F32)),
        grid=(B, dilation // group),
        in_specs=[spec, spec, spec],
        out_specs=(spec, spec),
        scratch_shapes=[pltpu.VMEM((3, A_HEADS, A_QBLOCK, A_KWIN), _F32)],
        compiler_params=pltpu.CompilerParams(dimension_semantics=("arbitrary", "arbitrary"),
                                             vmem_limit_bytes=VMEM_LIMIT),
        name=f"dilated_d{dilation}",
    )(q, k, v)


def _windowed_kernel(sink_ref, q_ref, k_ref, v_ref, o_ref, bias_ref, *, seq, slopes):
    nblk = seq // B_QBLOCK

    @pl.when((pl.program_id(0) == 0) & (pl.program_id(1) == 0))
    def _():
        rel0 = (lax.broadcasted_iota(jnp.int32, (B_QBLOCK, B_KWIN), 1)
                - lax.broadcasted_iota(jnp.int32, (B_QBLOCK, B_KWIN), 0))
        for case, shift in enumerate((0, -B_WINDOW, -2 * B_WINDOW)):
            dist = jnp.abs(rel0 + shift)
            for idx, head in enumerate(B_HEAD_ORDER):
                bias_ref[case, idx] = jnp.where(dist <= B_WINDOW, dist.astype(_F32) * (-slopes[head] * LOG2E),
                                                NEG_INF)

    low = lax.broadcasted_iota(jnp.int32, (B_QBLOCK, LANES), 1) < HEAD_DIM
    npair = B_HEADS // 2
    blocks = []
    for j in range(B_QBLOCKS_PER_STEP):
        blk = pl.program_id(1) * B_QBLOCKS_PER_STEP + j
        t0 = pl.multiple_of(blk * B_QBLOCK, B_QBLOCK)
        start = pl.multiple_of(jnp.clip(t0 - B_WINDOW, 0, seq - B_KWIN), B_WINDOW)
        case = jnp.where(blk == 0, 0, jnp.where(blk == nblk - 1, 2, 1))
        rows = slice(j * B_QBLOCK, (j + 1) * B_QBLOCK)
        blocks.append((rows, case, k_ref[0, pl.ds(start, B_KWIN), :], v_ref[0, pl.ds(start, B_KWIN), :]))

    def scores(item):
        (rows, _, kw, _), pair = item
        q = q_ref[0, rows, pair * LANES:(pair + 1) * LANES]
        zero = jnp.zeros_like(q)
        return _dot_nt(jnp.concatenate([jnp.where(low, q, zero), jnp.where(low, zero, q)], axis=0), kw)

    items = [(block, pair) for block in blocks for pair in range(npair)]
    s_next = scores(items[0])
    for n, ((rows, case, _, vw), pair) in enumerate(items):
        s_pair, s_next = s_next, (scores(items[n + 1]) if n + 1 < len(items) else None)
        ps, factors = [], []
        for side in range(2):
            idx = 2 * pair + side
            s = s_pair[side * B_QBLOCK:(side + 1) * B_QBLOCK] + bias_ref[case, idx]
            m = jnp.max(s, axis=-1, keepdims=True)
            p = jnp.exp2(s - m)
            l = jnp.sum(p, axis=-1, keepdims=True)
            sk = sink_ref[B_HEAD_ORDER[idx]] * LOG2E
            big = jnp.maximum(m, sk)
            e = jnp.exp2(m - big)
            factors.append(e / (l * e + jnp.exp2(sk - big)))
            ps.append(p.astype(_BF16))
        acc = _dot(jnp.concatenate(ps, axis=0), vw)
        out = jnp.where(low, acc[:B_QBLOCK] * factors[0], acc[B_QBLOCK:] * factors[1])
        o_ref[0, rows, pair * LANES:(pair + 1) * LANES] = out.astype(o_ref.dtype)


def _windowed(qb, kb, vb, sink, slopes):
    B, S, _ = qb.shape
    kv_spec = pl.BlockSpec((1, S, B_KV_WIDTH), lambda b, i, sink: (b, 0, 0))
    q_spec = pl.BlockSpec((1, B_QSTEP, B_WIDTH), lambda b, i, sink: (b, i, 0))
    return pl.pallas_call(
        functools.partial(_windowed_kernel, seq=S, slopes=slopes),
        out_shape=jax.ShapeDtypeStruct((B, S, B_WIDTH), _BF16),
        grid_spec=pltpu.PrefetchScalarGridSpec(
            num_scalar_prefetch=1, grid=(B, S // B_QSTEP),
            in_specs=[q_spec, kv_spec, kv_spec], out_specs=q_spec,
            scratch_shapes=[pltpu.VMEM((3, B_HEADS, B_QBLOCK, B_KWIN), _F32)]),
        compiler_params=pltpu.CompilerParams(dimension_semantics=("arbitrary", "arbitrary"),
                                             vmem_limit_bytes=VMEM_LIMIT),
        name="windowed_gqa",
    )(sink, qb, kb, vb)


def _latent_kernel(q_ref, k_ref, vt_ref, o_ref, *, seq):
    nkb = seq // C_KBLOCK
    heads = range(C_PAIR)

    for qb in range(C_QBLOCKS_PER_STEP):
        rows = slice(qb * C_QBLOCK, (qb + 1) * C_QBLOCK)
        qs = [q_ref[0, hd, rows, :] for hd in heads]

        def scores(hd, j):
            return _dot_nt(k_ref[0, hd, j * C_KBLOCK:(j + 1) * C_KBLOCK, :], qs[hd])

        ms = [jnp.full((1, C_QBLOCK), NEG_INF, _F32) for _ in heads]
        accs = [jnp.zeros((C_VROWS, C_QBLOCK), _F32) for _ in heads]
        s_next = [scores(hd, 0) for hd in heads]
        for j in range(nkb):
            s_cur, s_next = s_next, ([scores(hd, j + 1) for hd in heads] if j + 1 < nkb else None)
            for hd in heads:
                m_new = jnp.maximum(ms[hd], jnp.max(s_cur[hd], axis=0, keepdims=True))
                alpha = jnp.exp2(ms[hd] - m_new)
                p = jnp.exp2(s_cur[hd] - m_new).astype(_BF16)
                accs[hd] = alpha * accs[hd] + _dot(vt_ref[0, hd, j], p)
                ms[hd] = m_new
        out_t = jnp.concatenate([accs[hd][:C_V] * (1.0 / accs[hd][C_V:C_V + 1]) for hd in heads], axis=0)
        o_ref[0, rows, :] = out_t.T.astype(o_ref.dtype)


def _latent(qc, kc, vt):
    B, H, S, _ = qc.shape
    nkb = S // C_KBLOCK
    return pl.pallas_call(
        functools.partial(_latent_kernel, seq=S),
        out_shape=jax.ShapeDtypeStruct((B, S, C_WIDTH), _BF16),
        grid=(B, H // C_PAIR, S // C_QSTEP),
        in_specs=[pl.BlockSpec((1, C_PAIR, C_QSTEP, C_PAD), lambda b, g, i: (b, g, i, 0)),
                  pl.BlockSpec((1, C_PAIR, S, C_PAD), lambda b, g, i: (b, g, 0, 0)),
                  pl.BlockSpec((1, C_PAIR, nkb, C_VROWS, C_KBLOCK), lambda b, g, i: (b, g, 0, 0, 0))],
        out_specs=pl.BlockSpec((1, C_QSTEP, C_PAIR * C_V), lambda b, g, i: (b, i, g)),
        compiler_params=pltpu.CompilerParams(dimension_semantics=("arbitrary", "arbitrary", "arbitrary"),
                                             vmem_limit_bytes=VMEM_LIMIT),
        name="latent_attention",
    )(qc, kc, vt)


def _out_kernel(x_ref, g_ref, o1_ref, o2_ref, o3_ref, l1_ref, l2_ref, l3_ref, yb_ref, yc_ref,
                w_o_ref, post_g_ref, out_ref, y_scr, pat_scr):
    outs, lses = [], []
    for p, ((_, d), o_ref, l_ref) in enumerate(zip(A_PATTERNS, (o1_ref, o2_ref, o3_ref), (l1_ref, l2_ref, l3_ref))):
        if d == 1:
            outs.append(o_ref[0].astype(_F32))
            lses.append(l_ref[0])
            continue
        rows = ROW_TILE // d
        nchunk = A_WIDTH // LANES
        for r in range(d):
            for c in range(nchunk):
                lanes = slice(r * A_WIDTH + c * LANES, r * A_WIDTH + (c + 1) * LANES)
                pat_scr[2 * p, c, pl.ds(r, rows, stride=d), :] = o_ref[0, :, lanes].astype(_F32)
                pat_scr[2 * p + 1, c, pl.ds(r, rows, stride=d), :] = l_ref[0, :, lanes]
        outs.append(jnp.concatenate([pat_scr[2 * p, c] for c in range(nchunk)], axis=-1))
        lses.append(jnp.concatenate([pat_scr[2 * p + 1, c] for c in range(nchunk)], axis=-1))
    l1, l2, l3 = lses
    big = jnp.maximum(jnp.maximum(l1, l2), l3)
    e1, e2, e3 = jnp.exp2(l1 - big), jnp.exp2(l2 - big), jnp.exp2(l3 - big)
    num = e1 * outs[0] + e2 * outs[1] + e3 * outs[2]
    ya = num / (e1 + e2 + e3)
    y_scr[:, :A_WIDTH] = (ya * g_ref[0, :, :A_WIDTH].astype(_F32)).astype(_BF16)
    y_scr[:, A_WIDTH:A_WIDTH + B_WIDTH] = yb_ref[0] * g_ref[0, :, A_WIDTH:A_WIDTH + B_WIDTH]
    y_scr[:, A_WIDTH + B_WIDTH:] = yc_ref[0] * g_ref[0, :, A_WIDTH + B_WIDTH:]
    y = _dot(y_scr[...], w_o_ref[...])
    out_ref[0] = x_ref[0] + _rms(y, post_g_ref[...])


def _out_proj(x, gates, o_pats, lse_pats, yb, yc, w_o, post_g):
    B, S, _ = x.shape
    row = lambda n: pl.BlockSpec((1, ROW_TILE, n), lambda b, i: (b, i, 0))
    full = lambda a: pl.BlockSpec(a.shape, lambda b, i: (0,) * a.ndim)
    return pl.pallas_call(
        _out_kernel,
        out_shape=jax.ShapeDtypeStruct(x.shape, x.dtype),
        grid=(B, S // ROW_TILE),
        in_specs=[row(D_MODEL), row(D_MIX)] + [_residue_view_rows(d) for _, d in A_PATTERNS] * 2
                 + [row(B_WIDTH), row(C_WIDTH), full(w_o), full(post_g)],
        out_specs=row(D_MODEL),
        scratch_shapes=[pltpu.VMEM((ROW_TILE, D_MIX), _BF16),
                        pltpu.VMEM((2 * len(A_PATTERNS), A_WIDTH // LANES, ROW_TILE, LANES), _F32)],
        compiler_params=pltpu.CompilerParams(dimension_semantics=("arbitrary", "arbitrary"),
                                             vmem_limit_bytes=VMEM_LIMIT),
        name="out_proj",
    )(x, gates, *o_pats, *lse_pats, yb, yc, w_o, post_g)


def _pair_heads(t, axis):
    parts = jnp.split(t, B_HEADS, axis=axis)
    return jnp.concatenate([parts[h] for h in B_HEAD_ORDER], axis=axis)


def _rot_cols(w):
    half = C_ROPE // 2
    return jnp.concatenate([-w[..., half:], w[..., :half]], axis=-1)


def _layer_weights(w_in, w_uq, w_ukv, w_o):
    o = 0
    cols = {}
    for name, n in (("qa", A_WIDTH), ("ka", A_WIDTH), ("va", A_WIDTH), ("ga", A_WIDTH),
                    ("qb", B_WIDTH), ("kb", B_KV_WIDTH), ("vb", B_KV_WIDTH), ("gb", B_WIDTH),
                    ("cq", Q_LORA), ("ckv", KV_LORA), ("kr", C_ROPE), ("gc", C_WIDTH)):
        cols[name] = w_in[:, o:o + n]
        o += n
    w_a = jnp.concatenate([cols["qa"], cols["ka"], cols["va"]], axis=1)
    w_g = jnp.concatenate([cols["ga"], _pair_heads(cols["gb"], 1), cols["gc"]], axis=1)
    w_b = jnp.concatenate([_pair_heads(cols["qb"], 1), cols["kb"], cols["vb"]], axis=1)
    w_c = jnp.concatenate([cols["cq"], cols["ckv"], jnp.zeros((D_MODEL, C_NOPE), w_in.dtype),
                           cols["kr"], _rot_cols(cols["kr"])], axis=1)
    uq = w_uq.reshape(Q_LORA, C_HEADS, C_NOPE + C_ROPE)
    uq = jnp.concatenate([uq, _rot_cols(uq[..., C_NOPE:])], axis=-1).reshape(Q_LORA, C_HEADS * C_PAD)
    ukv = w_ukv.reshape(KV_LORA, C_HEADS, C_NOPE + C_V)
    uk = jnp.concatenate([ukv[..., :C_NOPE], jnp.zeros((KV_LORA, C_HEADS, C_PAD - C_NOPE), w_ukv.dtype)],
                         axis=-1).reshape(KV_LORA, C_HEADS * C_PAD)
    uvt = jnp.transpose(ukv[..., C_NOPE:], (1, 2, 0))
    uvt = jnp.concatenate([uvt, jnp.zeros((C_HEADS, C_VROWS - C_V, KV_LORA), w_ukv.dtype)], axis=1)
    uvt = uvt.reshape(C_HEADS * C_VROWS, KV_LORA)

    w_o_p = jnp.concatenate([w_o[:A_WIDTH], _pair_heads(w_o[A_WIDTH:A_WIDTH + B_WIDTH], 0),
                             w_o[A_WIDTH + B_WIDTH:]], axis=0)
    bf = lambda t: t.astype(_BF16)
    return (bf(w_a), bf(w_g), bf(w_b), bf(w_c), bf(uq), bf(uk), bf(uvt)), bf(w_o_p)


def kernel(x, positions, pre_norm, w_in, q_a_norm, kv_a_norm, w_uq, w_ukv, sink, w_o, post_norm):
    depth = w_in.shape[0]
    slopes_a, slopes_b = _alibi_slopes()
    cos_t, sin_t = _rope_tables(positions)
    for i in range(depth):
        in_w, w_o_p = _layer_weights(w_in[i], w_uq[i], w_ukv[i], w_o[i])
        qkv_a, (gates, qb, kb, vb, qc, kc, vt) = _in_proj(
            x, cos_t, sin_t, pre_norm[i][None], q_a_norm[i][None], kv_a_norm[i][None], *in_w)
        pats = [_dilated_pattern(*qkv, d, slopes_a) for qkv, (_, d) in zip(qkv_a, A_PATTERNS)]
        yb = _windowed(qb, kb, vb, sink[i], slopes_b)
        yc = _latent(qc, kc, vt)
        x = _out_proj(x, gates, [p[0] for p in pats], [p[1] for p in pats], yb, yc, w_o_p, post_norm[i][None])
    return x
```

```python
import functools
import math

import numpy as np
import jax
import jax.numpy as jnp
from jax import lax
from jax.experimental import pallas as pl
from jax.experimental.pallas import tpu as pltpu

D_MODEL = 1024
HEAD_DIM = 64
A_HEADS = 4
A_WIDTH = A_HEADS * HEAD_DIM
A_PATTERNS = ((128, 1), (512, 4), (2048, 16))
A_HALF = 64
B_HEADS = 6
B_KV_HEADS = 2
B_GROUP = B_HEADS // B_KV_HEADS
B_WIDTH = B_HEADS * HEAD_DIM
B_KV_WIDTH = B_KV_HEADS * HEAD_DIM
B_WINDOW = 128
C_HEADS = 6
C_NOPE = 64
C_ROPE = 32
C_V = 64
C_WIDTH = C_HEADS * C_V
Q_LORA = 256
KV_LORA = 128
ROPE_THETA = 10000.0
D_MIX = A_WIDTH + B_WIDTH + C_WIDTH
N_ALIBI = A_HEADS + B_HEADS
RMS_EPS = 1e-6
NEG_INF = -1e30
LOG2E = math.log2(math.e)

LANES = 128
C_PAD = LANES
C_VROWS = 80
C_PAIR = 2

ROW_TILE = 1024
A_QBLOCK = 128
A_KWIN = A_QBLOCK + 2 * A_HALF
A_RESIDUES_PER_STEP = 4
A_LOOP_UNROLL = 2
B_QBLOCK = 128
B_KWIN = B_QBLOCK + 2 * B_WINDOW
B_QBLOCKS_PER_STEP = 4
B_QSTEP = B_QBLOCKS_PER_STEP * B_QBLOCK
C_QBLOCK = 512
C_KBLOCK = 256
C_QBLOCKS_PER_STEP = 2
C_QSTEP = C_QBLOCKS_PER_STEP * C_QBLOCK
VMEM_LIMIT = 48 * 1024 * 1024

B_HEAD_ORDER = (0, 3, 1, 4, 2, 5)

_F32 = jnp.float32
_BF16 = jnp.bfloat16


def _alibi_slopes():
    s = 2.0 ** (-8.0 * np.arange(1, N_ALIBI + 1, dtype=np.float64) / N_ALIBI)
    return [float(v) for v in s[B_HEADS:]], [float(v) for v in s[:B_HEADS]]


def _rms(x, g):
    return x * lax.rsqrt(jnp.mean(x * x, axis=-1, keepdims=True) + RMS_EPS) * g


def _residue_view_shape(batch, seq, d, dtype):
    return jax.ShapeDtypeStruct((batch, seq // d, d * A_WIDTH), dtype)


def _layer_block(stacked, layer):
    return pl.BlockSpec((1,) + stacked.shape[1:], lambda b, i: (layer,) + (0,) * (stacked.ndim - 1))


def _residue_view_rows(d):
    return pl.BlockSpec((1, ROW_TILE // d, d * A_WIDTH), lambda b, i: (b, i, 0))


def _dot(a, b):
    return jnp.dot(a, b, preferred_element_type=_F32)


def _dot_nt(a, b):
    return lax.dot_general(a, b, (((1,), (1,)), ((), ())), preferred_element_type=_F32)


def _rope_table_kernel(pos_ref, freq_ref, cos_ref, sin_ref):
    ang = freq_ref[...] * pos_ref[0].astype(_F32)
    cos, sin = jnp.cos(ang), jnp.sin(ang)
    tokens = ang.shape[1]
    ones = jnp.ones((C_NOPE, tokens), _F32)
    zeros = jnp.zeros((C_NOPE, tokens), _F32)
    tail = zeros[:C_PAD - C_NOPE - C_ROPE]
    cos_ref[0] = jnp.concatenate([ones, cos, cos, tail], axis=0).T
    sin_ref[0] = jnp.concatenate([zeros, sin, sin, tail], axis=0).T


def _rope_tables(positions):
    B, S = positions.shape
    half = C_ROPE // 2
    freq = ROPE_THETA ** (-2.0 * jnp.arange(half, dtype=_F32) / C_ROPE)
    out = jax.ShapeDtypeStruct((B, S, C_PAD), _F32)
    return pl.pallas_call(
        _rope_table_kernel,
        out_shape=(out, out),
        grid=(B, S // ROW_TILE),
        in_specs=[pl.BlockSpec((1, 1, ROW_TILE), lambda b, i: (b, 0, i)),
                  pl.BlockSpec((half, 1), lambda b, i: (0, 0))],
        out_specs=(pl.BlockSpec((1, ROW_TILE, C_PAD), lambda b, i: (b, i, 0)),
                   pl.BlockSpec((1, ROW_TILE, C_PAD), lambda b, i: (b, i, 0))),
        compiler_params=pltpu.CompilerParams(dimension_semantics=("arbitrary", "arbitrary")),
        name="rope_tables",
    )(positions.reshape(B, 1, S), freq.reshape(half, 1))


def _in_proj_kernel(x_ref, cos_ref, sin_ref, pre_g_ref, qn_g_ref, kvn_g_ref,
                    w_a_ref, w_g_ref, w_b_ref, w_c_ref, w_uq_ref, w_uk_ref, w_uvt_ref,
                    *rest):
    a_refs, (g_ref, qb_ref, kb_ref, vb_ref, qc_ref, kc_ref, vt_ref, a_scr) = rest[:3 * len(A_PATTERNS)], rest[-8:]
    h = _rms(x_ref[0], pre_g_ref[0]).astype(_BF16)
    score_scale = HEAD_DIM ** -0.5 * LOG2E

    pa = _dot(h, w_a_ref[0])
    nchunk = 3 * A_WIDTH // LANES
    for c in range(nchunk):
        chunk = pa[:, c * LANES:(c + 1) * LANES]
        a_scr[c] = chunk * score_scale if c < A_WIDTH // LANES else chunk
    for p, (_, d) in enumerate(A_PATTERNS):
        rows = ROW_TILE // d
        for r in range(d):
            for c in range(nchunk):
                res = a_scr[c, pl.ds(r, rows, stride=d), :] if d > 1 else a_scr[c]
                lane0 = r * A_WIDTH + (c * LANES) % A_WIDTH
                a_refs[3 * p + c * LANES // A_WIDTH][0, :, lane0:lane0 + LANES] = res.astype(_BF16)

    for c in range(0, D_MIX, 256):
        g = _dot(h, w_g_ref[0, :, c:c + 256])
        g_ref[0, :, c:c + 256] = (g / (1.0 + jnp.exp(-g))).astype(_BF16)

    pb = _dot(h, w_b_ref[0])
    qb_ref[0] = (pb[:, :B_WIDTH] * score_scale).astype(_BF16)
    kb_ref[0] = pb[:, B_WIDTH:B_WIDTH + B_KV_WIDTH].astype(_BF16)
    vb_ref[0] = pb[:, B_WIDTH + B_KV_WIDTH:].astype(_BF16)

    pc = _dot(h, w_c_ref[0])
    cos = cos_ref[0]
    sin = sin_ref[0]

    def rotary(t):
        return t * cos + pltpu.roll(t, C_PAD - C_ROPE, axis=1) * sin

    cq = _rms(pc[:, :Q_LORA], qn_g_ref[0]).astype(_BF16)
    ckv = _rms(pc[:, Q_LORA:Q_LORA + KV_LORA], kvn_g_ref[0]).astype(_BF16)
    k_rope = rotary(pc[:, Q_LORA + KV_LORA:])
    c_scale = (C_NOPE + C_ROPE) ** -0.5 * LOG2E
    for hd in range(C_HEADS):
        cols = slice(hd * C_PAD, (hd + 1) * C_PAD)
        qc_ref[0, hd] = (rotary(_dot(cq, w_uq_ref[0, :, cols])) * c_scale).astype(_BF16)
        kc_ref[0, hd] = (_dot(ckv, w_uk_ref[0, :, cols]) + k_rope).astype(_BF16)
    vt = _dot_nt(w_uvt_ref[0], ckv)
    row = lax.broadcasted_iota(jnp.int32, vt.shape, 0)
    ones_row = functools.reduce(jnp.logical_or, [row == hd * C_VROWS + C_V for hd in range(C_HEADS)])
    vt = jnp.where(ones_row, 1.0, vt).astype(_BF16)
    for hd in range(C_HEADS):
        for c in range(ROW_TILE // C_KBLOCK):
            vt_ref[0, hd, c] = vt[hd * C_VROWS:(hd + 1) * C_VROWS, c * C_KBLOCK:(c + 1) * C_KBLOCK]


def _in_proj(layer, x, cos_t, sin_t, pre_g, qn_g, kvn_g, w_a, w_g, w_b, w_c, w_uq, w_uk, w_uvt):
    B, S, _ = x.shape
    nblk = S // ROW_TILE
    row = lambda n: pl.BlockSpec((1, ROW_TILE, n), lambda b, i: (b, i, 0))
    full = lambda a: _layer_block(a, layer)
    bf = lambda n: jax.ShapeDtypeStruct((B, S, n), _BF16)
    head4 = pl.BlockSpec((1, C_HEADS, ROW_TILE, C_PAD), lambda b, i: (b, 0, i, 0))
    weights = (pre_g, qn_g, kvn_g, w_a, w_g, w_b, w_c, w_uq, w_uk, w_uvt)
    a_shapes = tuple(_residue_view_shape(B, S, d, _BF16) for _, d in A_PATTERNS for _ in range(3))
    a_specs = tuple(_residue_view_rows(d) for _, d in A_PATTERNS for _ in range(3))
    outs = pl.pallas_call(
        _in_proj_kernel,
        out_shape=a_shapes + (bf(D_MIX), bf(B_WIDTH), bf(B_KV_WIDTH), bf(B_KV_WIDTH),
                              jax.ShapeDtypeStruct((B, C_HEADS, S, C_PAD), _BF16),
                              jax.ShapeDtypeStruct((B, C_HEADS, S, C_PAD), _BF16),
                              jax.ShapeDtypeStruct((B, C_HEADS, S // C_KBLOCK, C_VROWS, C_KBLOCK), _BF16)),
        grid=(B, nblk),
        in_specs=[row(D_MODEL), row(C_PAD), row(C_PAD)] + [full(w) for w in weights],
        out_specs=a_specs + (row(D_MIX), row(B_WIDTH), row(B_KV_WIDTH), row(B_KV_WIDTH), head4, head4,
                             pl.BlockSpec((1, C_HEADS, ROW_TILE // C_KBLOCK, C_VROWS, C_KBLOCK),
                                          lambda b, i: (b, 0, i, 0, 0))),
        scratch_shapes=[pltpu.VMEM((3 * A_WIDTH // LANES, ROW_TILE, LANES), _F32)],
        compiler_params=pltpu.CompilerParams(dimension_semantics=("arbitrary", "arbitrary"),
                                             vmem_limit_bytes=VMEM_LIMIT),
        name="in_proj",
    )(x, cos_t, sin_t, *weights)
    n_a = 3 * len(A_PATTERNS)
    return [outs[3 * p:3 * p + 3] for p in range(len(A_PATTERNS))], outs[n_a:]


def _dilated_kernel(q_ref, k_ref, v_ref, o_ref, lse_ref, bias_ref, *, length, dilation, group, slopes):
    nblk = length // A_QBLOCK
    lane_head = lax.shift_right_logical(lax.broadcasted_iota(jnp.int32, (A_QBLOCK, A_WIDTH), 1),
                                        int(math.log2(HEAD_DIM)))

    @pl.when((pl.program_id(0) == 0) & (pl.program_id(1) == 0))
    def _():
        rel0 = (lax.broadcasted_iota(jnp.int32, (A_QBLOCK, A_KWIN), 1)
                - lax.broadcasted_iota(jnp.int32, (A_QBLOCK, A_KWIN), 0))
        for case, shift in enumerate((0, -A_HALF, -2 * A_HALF)):
            dist = jnp.abs(rel0 + shift)
            for hd in range(A_HEADS):
                bias_ref[case, hd] = jnp.where(dist <= A_HALF,
                                               dist.astype(_F32) * (-slopes[hd] * dilation * LOG2E), NEG_INF)

    low = lax.broadcasted_iota(jnp.int32, (A_QBLOCK, LANES), 1) < HEAD_DIM

    def load(blk, lanes):
        t0 = pl.multiple_of(blk * A_QBLOCK, A_QBLOCK)
        start = pl.multiple_of(jnp.clip(t0 - A_HALF, 0, length - A_KWIN), A_HALF)
        case = jnp.where(blk == 0, 0, jnp.where(blk == nblk - 1, 2, 1))
        q = q_ref[0, pl.ds(t0, A_QBLOCK), lanes]
        q_heads = jnp.concatenate([jnp.where(lane_head == hd, q, jnp.zeros_like(q)) for hd in range(A_HEADS)],
                                  axis=0)
        return t0, case, q_heads, k_ref[0, pl.ds(start, A_KWIN), lanes], v_ref[0, pl.ds(start, A_KWIN), lanes]

    def softmax(s_all, case):
        ps, ms, ls = [], [], []
        for hd in range(A_HEADS):
            s = s_all[hd * A_QBLOCK:(hd + 1) * A_QBLOCK] + bias_ref[case, hd]
            m = jnp.max(s, axis=-1, keepdims=True)
            p = jnp.exp2(s - m)
            ls.append(jnp.sum(p, axis=-1, keepdims=True))
            ms.append(m)
            ps.append(p.astype(_BF16))
        return ps, ms, ls

    def values(t0, lanes, ps, ms, ls, vw):
        outs, lses = [], []
        for pair in range(A_HEADS // 2):
            h0, h1 = 2 * pair, 2 * pair + 1
            acc = _dot(jnp.concatenate([ps[h0], ps[h1]], axis=0), vw[:, pair * LANES:(pair + 1) * LANES])
            outs.append(jnp.where(low, acc[:A_QBLOCK] * (1.0 / ls[h0]), acc[A_QBLOCK:] * (1.0 / ls[h1])))
            lses.append(jnp.where(low, ms[h0] + jnp.log2(ls[h0]), ms[h1] + jnp.log2(ls[h1])))
        o_ref[0, pl.ds(t0, A_QBLOCK), lanes] = jnp.concatenate(outs, axis=-1).astype(o_ref.dtype)
        lse_ref[0, pl.ds(t0, A_QBLOCK), lanes] = jnp.concatenate(lses, axis=-1)

    def two_blocks(i, lanes):
        blocks = [load(2 * i + j, lanes) for j in range(2)]
        scores = [_dot_nt(q_heads, kw) for _, _, q_heads, kw, _ in blocks]
        for (t0, case, _, _, vw), s_all in zip(blocks, scores):
            values(t0, lanes, *softmax(s_all, case), vw)

    def residue_class(lanes):
        def body(i, carry):
            two_blocks(i, lanes)
            return carry
        lax.fori_loop(0, nblk // 2, body, 0, unroll=min(nblk // 2, A_LOOP_UNROLL))

    for g in range(group):
        residue_class(slice(g * A_WIDTH, (g + 1) * A_WIDTH))


def _dilated_pattern(q, k, v, dilation, slopes):
    B, length, _ = q.shape
    group = min(dilation, A_RESIDUES_PER_STEP)
    spec = pl.BlockSpec((1, length, group * A_WIDTH), lambda b, r: (b, 0, r))
    return pl.pallas_call(
        functools.partial(_dilated_kernel, length=length, dilation=dilation, group=group, slopes=slopes),
        out_shape=(jax.ShapeDtypeStruct(q.shape, _BF16), jax.ShapeDtypeStruct(q.shape, _F32)),
        grid=(B, dilation // group),
        in_specs=[spec, spec, spec],
        out_specs=(spec, spec),
        scratch_shapes=[pltpu.VMEM((3, A_HEADS, A_QBLOCK, A_KWIN), _F32)],
        compiler_params=pltpu.CompilerParams(dimension_semantics=("arbitrary", "arbitrary"),
                                             vmem_limit_bytes=VMEM_LIMIT),
        name=f"dilated_d{dilation}",
    )(q, k, v)


def _windowed_kernel(sink_ref, q_ref, k_ref, v_ref, o_ref, bias_ref, *, seq, slopes):
    nblk = seq // B_QBLOCK

    @pl.when((pl.program_id(0) == 0) & (pl.program_id(1) == 0))
    def _():
        rel0 = (lax.broadcasted_iota(jnp.int32, (B_QBLOCK, B_KWIN), 1)
                - lax.broadcasted_iota(jnp.int32, (B_QBLOCK, B_KWIN), 0))
        for case, shift in enumerate((0, -B_WINDOW, -2 * B_WINDOW)):
            dist = jnp.abs(rel0 + shift)
            for idx, head in enumerate(B_HEAD_ORDER):
                bias_ref[case, idx] = jnp.where(dist <= B_WINDOW, dist.astype(_F32) * (-slopes[head] * LOG2E),
                                                NEG_INF)

    low = lax.broadcasted_iota(jnp.int32, (B_QBLOCK, LANES), 1) < HEAD_DIM
    npair = B_HEADS // 2
    blocks = []
    for j in range(B_QBLOCKS_PER_STEP):
        blk = pl.program_id(1) * B_QBLOCKS_PER_STEP + j
        t0 = pl.multiple_of(blk * B_QBLOCK, B_QBLOCK)
        start = pl.multiple_of(jnp.clip(t0 - B_WINDOW, 0, seq - B_KWIN), B_WINDOW)
        case = jnp.where(blk == 0, 0, jnp.where(blk == nblk - 1, 2, 1))
        rows = slice(j * B_QBLOCK, (j + 1) * B_QBLOCK)
        blocks.append((rows, case, k_ref[0, pl.ds(start, B_KWIN), :], v_ref[0, pl.ds(start, B_KWIN), :]))

    def scores(item):
        (rows, _, kw, _), pair = item
        q = q_ref[0, rows, pair * LANES:(pair + 1) * LANES]
        zero = jnp.zeros_like(q)
        return _dot_nt(jnp.concatenate([jnp.where(low, q, zero), jnp.where(low, zero, q)], axis=0), kw)

    items = [(block, pair) for block in blocks for pair in range(npair)]
    s_next = scores(items[0])
    for n, ((rows, case, _, vw), pair) in enumerate(items):
        s_pair, s_next = s_next, (scores(items[n + 1]) if n + 1 < len(items) else None)
        ps, factors = [], []
        for side in range(2):
            idx = 2 * pair + side
            s = s_pair[side * B_QBLOCK:(side + 1) * B_QBLOCK] + bias_ref[case, idx]
            m = jnp.max(s, axis=-1, keepdims=True)
            p = jnp.exp2(s - m)
            l = jnp.sum(p, axis=-1, keepdims=True)
            factors.append(1.0 / (l + jnp.exp2(sink_ref[B_HEAD_ORDER[idx]] * LOG2E - m)))
            ps.append(p.astype(_BF16))
        acc = _dot(jnp.concatenate(ps, axis=0), vw)
        out = jnp.where(low, acc[:B_QBLOCK] * factors[0], acc[B_QBLOCK:] * factors[1])
        o_ref[0, rows, pair * LANES:(pair + 1) * LANES] = out.astype(o_ref.dtype)


def _windowed(qb, kb, vb, sink, slopes):
    B, S, _ = qb.shape
    kv_spec = pl.BlockSpec((1, S, B_KV_WIDTH), lambda b, i, sink: (b, 0, 0))
    q_spec = pl.BlockSpec((1, B_QSTEP, B_WIDTH), lambda b, i, sink: (b, i, 0))
    return pl.pallas_call(
        functools.partial(_windowed_kernel, seq=S, slopes=slopes),
        out_shape=jax.ShapeDtypeStruct((B, S, B_WIDTH), _BF16),
        grid_spec=pltpu.PrefetchScalarGridSpec(
            num_scalar_prefetch=1, grid=(B, S // B_QSTEP),
            in_specs=[q_spec, kv_spec, kv_spec], out_specs=q_spec,
            scratch_shapes=[pltpu.VMEM((3, B_HEADS, B_QBLOCK, B_KWIN), _F32)]),
        compiler_params=pltpu.CompilerParams(dimension_semantics=("arbitrary", "arbitrary"),
                                             vmem_limit_bytes=VMEM_LIMIT),
        name="windowed_gqa",
    )(sink, qb, kb, vb)


def _latent_kernel(q_ref, k_ref, vt_ref, o_ref, *, seq):
    nkb = seq // C_KBLOCK
    heads = range(C_PAIR)

    for qb in range(C_QBLOCKS_PER_STEP):
        rows = slice(qb * C_QBLOCK, (qb + 1) * C_QBLOCK)
        qs = [q_ref[0, hd, rows, :] for hd in heads]

        def scores(hd, j):
            return _dot_nt(k_ref[0, hd, j * C_KBLOCK:(j + 1) * C_KBLOCK, :], qs[hd])

        ms = [jnp.full((1, C_QBLOCK), NEG_INF, _F32) for _ in heads]
        accs = [jnp.zeros((C_VROWS, C_QBLOCK), _F32) for _ in heads]
        s_next = [scores(hd, 0) for hd in heads]
        for j in range(nkb):
            s_cur, s_next = s_next, ([scores(hd, j + 1) for hd in heads] if j + 1 < nkb else None)
            for hd in heads:
                m_new = jnp.maximum(ms[hd], jnp.max(s_cur[hd], axis=0, keepdims=True))
                alpha = jnp.exp2(ms[hd] - m_new)
                p = jnp.exp2(s_cur[hd] - m_new).astype(_BF16)
                accs[hd] = alpha * accs[hd] + _dot(vt_ref[0, hd, j], p)
                ms[hd] = m_new
        out_t = jnp.concatenate([accs[hd][:C_V] * (1.0 / accs[hd][C_V:C_V + 1]) for hd in heads], axis=0)
        o_ref[0, rows, :] = out_t.T.astype(o_ref.dtype)


def _latent(qc, kc, vt):
    B, H, S, _ = qc.shape
    nkb = S // C_KBLOCK
    return pl.pallas_call(
        functools.partial(_latent_kernel, seq=S),
        out_shape=jax.ShapeDtypeStruct((B, S, C_WIDTH), _BF16),
        grid=(B, H // C_PAIR, S // C_QSTEP),
        in_specs=[pl.BlockSpec((1, C_PAIR, C_QSTEP, C_PAD), lambda b, g, i: (b, g, i, 0)),
                  pl.BlockSpec((1, C_PAIR, S, C_PAD), lambda b, g, i: (b, g, 0, 0)),
                  pl.BlockSpec((1, C_PAIR, nkb, C_VROWS, C_KBLOCK), lambda b, g, i: (b, g, 0, 0, 0))],
        out_specs=pl.BlockSpec((1, C_QSTEP, C_PAIR * C_V), lambda b, g, i: (b, i, g)),
        compiler_params=pltpu.CompilerParams(dimension_semantics=("arbitrary", "arbitrary", "arbitrary"),
                                             vmem_limit_bytes=VMEM_LIMIT),
        name="latent_attention",
    )(qc, kc, vt)


def _out_kernel(x_ref, g_ref, o1_ref, o2_ref, o3_ref, l1_ref, l2_ref, l3_ref, yb_ref, yc_ref,
                w_o_ref, post_g_ref, out_ref, y_scr, pat_scr):
    outs, lses = [], []
    for p, ((_, d), o_ref, l_ref) in enumerate(zip(A_PATTERNS, (o1_ref, o2_ref, o3_ref), (l1_ref, l2_ref, l3_ref))):
        if d == 1:
            outs.append(o_ref[0].astype(_F32))
            lses.append(l_ref[0])
            continue
        rows = ROW_TILE // d
        nchunk = A_WIDTH // LANES
        for r in range(d):
            for c in range(nchunk):
                lanes = slice(r * A_WIDTH + c * LANES, r * A_WIDTH + (c + 1) * LANES)
                pat_scr[2 * p, c, pl.ds(r, rows, stride=d), :] = o_ref[0, :, lanes].astype(_F32)
                pat_scr[2 * p + 1, c, pl.ds(r, rows, stride=d), :] = l_ref[0, :, lanes]
        outs.append(jnp.concatenate([pat_scr[2 * p, c] for c in range(nchunk)], axis=-1))
        lses.append(jnp.concatenate([pat_scr[2 * p + 1, c] for c in range(nchunk)], axis=-1))
    l1, l2, l3 = lses
    big = jnp.maximum(jnp.maximum(l1, l2), l3)
    e1, e2, e3 = jnp.exp2(l1 - big), jnp.exp2(l2 - big), jnp.exp2(l3 - big)
    num = e1 * outs[0] + e2 * outs[1] + e3 * outs[2]
    ya = num / (e1 + e2 + e3)
    y_scr[:, :A_WIDTH] = (ya * g_ref[0, :, :A_WIDTH].astype(_F32)).astype(_BF16)
    y_scr[:, A_WIDTH:A_WIDTH + B_WIDTH] = yb_ref[0] * g_ref[0, :, A_WIDTH:A_WIDTH + B_WIDTH]
    y_scr[:, A_WIDTH + B_WIDTH:] = yc_ref[0] * g_ref[0, :, A_WIDTH + B_WIDTH:]
    y = _dot(y_scr[...], w_o_ref[0])
    out_ref[0] = x_ref[0] + _rms(y, post_g_ref[0])


def _out_proj(layer, x, gates, o_pats, lse_pats, yb, yc, w_o, post_g):
    B, S, _ = x.shape
    row = lambda n: pl.BlockSpec((1, ROW_TILE, n), lambda b, i: (b, i, 0))
    full = lambda a: _layer_block(a, layer)
    return pl.pallas_call(
        _out_kernel,
        out_shape=jax.ShapeDtypeStruct(x.shape, x.dtype),
        grid=(B, S // ROW_TILE),
        in_specs=[row(D_MODEL), row(D_MIX)] + [_residue_view_rows(d) for _, d in A_PATTERNS] * 2
                 + [row(B_WIDTH), row(C_WIDTH), full(w_o), full(post_g)],
        out_specs=row(D_MODEL),
        scratch_shapes=[pltpu.VMEM((ROW_TILE, D_MIX), _BF16),
                        pltpu.VMEM((2 * len(A_PATTERNS), A_WIDTH // LANES, ROW_TILE, LANES), _F32)],
        compiler_params=pltpu.CompilerParams(dimension_semantics=("arbitrary", "arbitrary"),
                                             vmem_limit_bytes=VMEM_LIMIT),
        name="out_proj",
    )(x, gates, *o_pats, *lse_pats, yb, yc, w_o, post_g)


def _pair_heads(t, axis):
    parts = jnp.split(t, B_HEADS, axis=axis)
    return jnp.concatenate([parts[h] for h in B_HEAD_ORDER], axis=axis)


def _rot_cols(w):
    half = C_ROPE // 2
    return jnp.concatenate([-w[..., half:], w[..., :half]], axis=-1)


def _stacked_weights(w_in, w_uq, w_ukv, w_o):
    depth = w_in.shape[0]
    o = 0
    cols = {}
    for name, n in (("qa", A_WIDTH), ("ka", A_WIDTH), ("va", A_WIDTH), ("ga", A_WIDTH),
                    ("qb", B_WIDTH), ("kb", B_KV_WIDTH), ("vb", B_KV_WIDTH), ("gb", B_WIDTH),
                    ("cq", Q_LORA), ("ckv", KV_LORA), ("kr", C_ROPE), ("gc", C_WIDTH)):
        cols[name] = w_in[..., o:o + n]
        o += n
    w_a = jnp.concatenate([cols["qa"], cols["ka"], cols["va"]], axis=-1)
    w_g = jnp.concatenate([cols["ga"], _pair_heads(cols["gb"], -1), cols["gc"]], axis=-1)
    w_b = jnp.concatenate([_pair_heads(cols["qb"], -1), cols["kb"], cols["vb"]], axis=-1)
    w_c = jnp.concatenate([cols["cq"], cols["ckv"], jnp.zeros((depth, D_MODEL, C_NOPE), w_in.dtype),
                           cols["kr"], _rot_cols(cols["kr"])], axis=-1)
    uq = w_uq.reshape(depth, Q_LORA, C_HEADS, C_NOPE + C_ROPE)
    uq = jnp.concatenate([uq, _rot_cols(uq[..., C_NOPE:])], axis=-1).reshape(depth, Q_LORA, C_HEADS * C_PAD)
    ukv = w_ukv.reshape(depth, KV_LORA, C_HEADS, C_NOPE + C_V)
    uk = jnp.concatenate([ukv[..., :C_NOPE], jnp.zeros((depth, KV_LORA, C_HEADS, C_PAD - C_NOPE), w_ukv.dtype)],
                         axis=-1).reshape(depth, KV_LORA, C_HEADS * C_PAD)
    uvt = jnp.transpose(ukv[..., C_NOPE:], (0, 2, 3, 1))
    uvt = jnp.concatenate([uvt, jnp.zeros((depth, C_HEADS, C_VROWS - C_V, KV_LORA), w_ukv.dtype)], axis=2)
    uvt = uvt.reshape(depth, C_HEADS * C_VROWS, KV_LORA)

    w_o_p = jnp.concatenate([w_o[:, :A_WIDTH], _pair_heads(w_o[:, A_WIDTH:A_WIDTH + B_WIDTH], 1),
                             w_o[:, A_WIDTH + B_WIDTH:]], axis=1)
    bf = lambda t: t.astype(_BF16)
    return (bf(w_a), bf(w_g), bf(w_b), bf(w_c), bf(uq), bf(uk), bf(uvt)), bf(w_o_p)


def kernel(x, positions, pre_norm, w_in, q_a_norm, kv_a_norm, w_uq, w_ukv, sink, w_o, post_norm):
    depth = w_in.shape[0]
    slopes_a, slopes_b = _alibi_slopes()
    cos_t, sin_t = _rope_tables(positions)
    in_w, w_o_p = _stacked_weights(w_in, w_uq, w_ukv, w_o)
    gains = [g[:, None, :] for g in (pre_norm, q_a_norm, kv_a_norm, post_norm)]
    for i in range(depth):
        qkv_a, (gates, qb, kb, vb, qc, kc, vt) = _in_proj(i, x, cos_t, sin_t, *gains[:3], *in_w)
        pats = [_dilated_pattern(*qkv, d, slopes_a) for qkv, (_, d) in zip(qkv_a, A_PATTERNS)]
        yb = _windowed(qb, kb, vb, sink[i], slopes_b)
        yc = _latent(qc, kc, vt)
        x = _out_proj(i, x, gates, [p[0] for p in pats], [p[1] for p in pats], yb, yc, w_o_p, gains[3])
    return x
```

```python
import functools
import math

import numpy as np
import jax
import jax.numpy as jnp
from jax import lax
from jax.experimental import pallas as pl
from jax.experimental.pallas import tpu as pltpu

D_MODEL = 1024
HEAD_DIM = 64
A_HEADS = 4
A_WIDTH = A_HEADS * HEAD_DIM
A_PATTERNS = ((128, 1), (512, 4), (2048, 16))
A_HALF = 64
B_HEADS = 6
B_KV_HEADS = 2
B_GROUP = B_HEADS // B_KV_HEADS
B_WIDTH = B_HEADS * HEAD_DIM
B_KV_WIDTH = B_KV_HEADS * HEAD_DIM
B_WINDOW = 128
C_HEADS = 6
C_NOPE = 64
C_ROPE = 32
C_V = 64
C_WIDTH = C_HEADS * C_V
Q_LORA = 256
KV_LORA = 128
ROPE_THETA = 10000.0
D_MIX = A_WIDTH + B_WIDTH + C_WIDTH
N_ALIBI = A_HEADS + B_HEADS
RMS_EPS = 1e-6
NEG_INF = -1e30
LOG2E = math.log2(math.e)

LANES = 128
C_PAD = LANES
C_VROWS = 80
C_PAIR = 2

ROW_TILE = 1024
A_QBLOCK = 128
A_KWIN = A_QBLOCK + 2 * A_HALF
A_RESIDUES_PER_STEP = 4
A_LOOP_UNROLL = 2
B_QBLOCK = 128
B_KWIN = B_QBLOCK + 2 * B_WINDOW
B_QBLOCKS_PER_STEP = 4
B_QSTEP = B_QBLOCKS_PER_STEP * B_QBLOCK
C_QBLOCK = 512
C_KBLOCK = 256
C_QBLOCKS_PER_STEP = 2
C_QSTEP = C_QBLOCKS_PER_STEP * C_QBLOCK
VMEM_LIMIT = 48 * 1024 * 1024

B_HEAD_ORDER = (0, 3, 1, 4, 2, 5)

_F32 = jnp.float32
_BF16 = jnp.bfloat16


def _alibi_slopes():
    s = 2.0 ** (-8.0 * np.arange(1, N_ALIBI + 1, dtype=np.float64) / N_ALIBI)
    return [float(v) for v in s[B_HEADS:]], [float(v) for v in s[:B_HEADS]]


def _rms(x, g):
    return x * lax.rsqrt(jnp.mean(x * x, axis=-1, keepdims=True) + RMS_EPS) * g


def _residue_view_shape(batch, seq, d, dtype):
    return jax.ShapeDtypeStruct((batch, seq // d, d * A_WIDTH), dtype)


def _layer_block(stacked, layer):
    return pl.BlockSpec((1,) + stacked.shape[1:], lambda b, i: (layer,) + (0,) * (stacked.ndim - 1))


def _residue_view_rows(d):
    return pl.BlockSpec((1, ROW_TILE // d, d * A_WIDTH), lambda b, i: (b, i, 0))


def _dot(a, b):
    return jnp.dot(a, b, preferred_element_type=_F32)


def _dot_nt(a, b):
    return lax.dot_general(a, b, (((1,), (1,)), ((), ())), preferred_element_type=_F32)


def _rope_table_kernel(pos_ref, freq_ref, cos_ref, sin_ref):
    ang = freq_ref[...] * pos_ref[0].astype(_F32)
    cos, sin = jnp.cos(ang), jnp.sin(ang)
    tokens = ang.shape[1]
    ones = jnp.ones((C_NOPE, tokens), _F32)
    zeros = jnp.zeros((C_NOPE, tokens), _F32)
    tail = zeros[:C_PAD - C_NOPE - C_ROPE]
    cos_ref[0] = jnp.concatenate([ones, cos, cos, tail], axis=0).T
    sin_ref[0] = jnp.concatenate([zeros, sin, sin, tail], axis=0).T


def _rope_tables(positions):
    B, S = positions.shape
    half = C_ROPE // 2
    freq = ROPE_THETA ** (-2.0 * jnp.arange(half, dtype=_F32) / C_ROPE)
    out = jax.ShapeDtypeStruct((B, S, C_PAD), _F32)
    return pl.pallas_call(
        _rope_table_kernel,
        out_shape=(out, out),
        grid=(B, S // ROW_TILE),
        in_specs=[pl.BlockSpec((1, 1, ROW_TILE), lambda b, i: (b, 0, i)),
                  pl.BlockSpec((half, 1), lambda b, i: (0, 0))],
        out_specs=(pl.BlockSpec((1, ROW_TILE, C_PAD), lambda b, i: (b, i, 0)),
                   pl.BlockSpec((1, ROW_TILE, C_PAD), lambda b, i: (b, i, 0))),
        compiler_params=pltpu.CompilerParams(dimension_semantics=("arbitrary", "arbitrary")),
        name="rope_tables",
    )(positions.reshape(B, 1, S), freq.reshape(half, 1))


def _in_proj_kernel(x_ref, cos_ref, sin_ref, pre_g_ref, qn_g_ref, kvn_g_ref,
                    w_a_ref, w_g_ref, w_b_ref, w_c_ref, w_uq_ref, w_uk_ref, w_uvt_ref,
                    *rest):
    a_refs, (g_ref, qb_ref, kb_ref, vb_ref, qc_ref, kc_ref, vt_ref, a_scr) = rest[:3 * len(A_PATTERNS)], rest[-8:]
    h = _rms(x_ref[0], pre_g_ref[0]).astype(_BF16)
    score_scale = HEAD_DIM ** -0.5 * LOG2E

    pa = _dot(h, w_a_ref[0])
    nchunk = 3 * A_WIDTH // LANES
    for c in range(nchunk):
        chunk = pa[:, c * LANES:(c + 1) * LANES]
        a_scr[c] = chunk * score_scale if c < A_WIDTH // LANES else chunk
    for p, (_, d) in enumerate(A_PATTERNS):
        rows = ROW_TILE // d
        for r in range(d):
            for c in range(nchunk):
                res = a_scr[c, pl.ds(r, rows, stride=d), :] if d > 1 else a_scr[c]
                lane0 = r * A_WIDTH + (c * LANES) % A_WIDTH
                a_refs[3 * p + c * LANES // A_WIDTH][0, :, lane0:lane0 + LANES] = res.astype(_BF16)

    for c in range(0, D_MIX, 256):
        g = _dot(h, w_g_ref[0, :, c:c + 256])
        g_ref[0, :, c:c + 256] = (g / (1.0 + jnp.exp(-g))).astype(_BF16)

    pb = _dot(h, w_b_ref[0])
    qb_ref[0] = (pb[:, :B_WIDTH] * score_scale).astype(_BF16)
    kb_ref[0] = pb[:, B_WIDTH:B_WIDTH + B_KV_WIDTH].astype(_BF16)
    vb_ref[0] = pb[:, B_WIDTH + B_KV_WIDTH:].astype(_BF16)

    pc = _dot(h, w_c_ref[0])
    cos = cos_ref[0]
    sin = sin_ref[0]

    def rotary(t):
        return t * cos + pltpu.roll(t, C_PAD - C_ROPE, axis=1) * sin

    cq = _rms(pc[:, :Q_LORA], qn_g_ref[0]).astype(_BF16)
    ckv = _rms(pc[:, Q_LORA:Q_LORA + KV_LORA], kvn_g_ref[0]).astype(_BF16)
    k_rope = rotary(pc[:, Q_LORA + KV_LORA:])
    c_scale = (C_NOPE + C_ROPE) ** -0.5 * LOG2E
    for hd in range(C_HEADS):
        cols = slice(hd * C_PAD, (hd + 1) * C_PAD)
        qc_ref[0, hd] = (rotary(_dot(cq, w_uq_ref[0, :, cols])) * c_scale).astype(_BF16)
        kc_ref[0, hd] = (_dot(ckv, w_uk_ref[0, :, cols]) + k_rope).astype(_BF16)
    vt = _dot_nt(w_uvt_ref[0], ckv)
    row = lax.broadcasted_iota(jnp.int32, vt.shape, 0)
    ones_row = functools.reduce(jnp.logical_or, [row == hd * C_VROWS + C_V for hd in range(C_HEADS)])
    vt = jnp.where(ones_row, 1.0, vt).astype(_BF16)
    for hd in range(C_HEADS):
        for c in range(ROW_TILE // C_KBLOCK):
            vt_ref[0, hd, c] = vt[hd * C_VROWS:(hd + 1) * C_VROWS, c * C_KBLOCK:(c + 1) * C_KBLOCK]


def _in_proj(layer, x, cos_t, sin_t, pre_g, qn_g, kvn_g, w_a, w_g, w_b, w_c, w_uq, w_uk, w_uvt):
    B, S, _ = x.shape
    nblk = S // ROW_TILE
    row = lambda n: pl.BlockSpec((1, ROW_TILE, n), lambda b, i: (b, i, 0))
    full = lambda a: _layer_block(a, layer)
    bf = lambda n: jax.ShapeDtypeStruct((B, S, n), _BF16)
    head4 = pl.BlockSpec((1, C_HEADS, ROW_TILE, C_PAD), lambda b, i: (b, 0, i, 0))
    weights = (pre_g, qn_g, kvn_g, w_a, w_g, w_b, w_c, w_uq, w_uk, w_uvt)
    a_shapes = tuple(_residue_view_shape(B, S, d, _BF16) for _, d in A_PATTERNS for _ in range(3))
    a_specs = tuple(_residue_view_rows(d) for _, d in A_PATTERNS for _ in range(3))
    outs = pl.pallas_call(
        _in_proj_kernel,
        out_shape=a_shapes + (bf(D_MIX), bf(B_WIDTH), bf(B_KV_WIDTH), bf(B_KV_WIDTH),
                              jax.ShapeDtypeStruct((B, C_HEADS, S, C_PAD), _BF16),
                              jax.ShapeDtypeStruct((B, C_HEADS, S, C_PAD), _BF16),
                              jax.ShapeDtypeStruct((B, C_HEADS, S // C_KBLOCK, C_VROWS, C_KBLOCK), _BF16)),
        grid=(B, nblk),
        in_specs=[row(D_MODEL), row(C_PAD), row(C_PAD)] + [full(w) for w in weights],
        out_specs=a_specs + (row(D_MIX), row(B_WIDTH), row(B_KV_WIDTH), row(B_KV_WIDTH), head4, head4,
                             pl.BlockSpec((1, C_HEADS, ROW_TILE // C_KBLOCK, C_VROWS, C_KBLOCK),
                                          lambda b, i: (b, 0, i, 0, 0))),
        scratch_shapes=[pltpu.VMEM((3 * A_WIDTH // LANES, ROW_TILE, LANES), _F32)],
        compiler_params=pltpu.CompilerParams(dimension_semantics=("arbitrary", "arbitrary"),
                                             vmem_limit_bytes=VMEM_LIMIT),
        name="in_proj",
    )(x, cos_t, sin_t, *weights)
    n_a = 3 * len(A_PATTERNS)
    return [outs[3 * p:3 * p + 3] for p in range(len(A_PATTERNS))], outs[n_a:]


def _dilated_kernel(q_ref, k_ref, v_ref, o_ref, lse_ref, bias_ref, *, length, dilation, group, slopes):
    nblk = length // A_QBLOCK
    lane_head = lax.shift_right_logical(lax.broadcasted_iota(jnp.int32, (A_QBLOCK, A_WIDTH), 1),
                                        int(math.log2(HEAD_DIM)))

    @pl.when((pl.program_id(0) == 0) & (pl.program_id(1) == 0))
    def _():
        rel0 = (lax.broadcasted_iota(jnp.int32, (A_QBLOCK, A_KWIN), 1)
                - lax.broadcasted_iota(jnp.int32, (A_QBLOCK, A_KWIN), 0))
        for case, shift in enumerate((0, -A_HALF, -2 * A_HALF)):
            dist = jnp.abs(rel0 + shift)
            for hd in range(A_HEADS):
                bias_ref[case, hd] = jnp.where(dist <= A_HALF,
                                               dist.astype(_F32) * (-slopes[hd] * dilation * LOG2E), NEG_INF)

    low = lax.broadcasted_iota(jnp.int32, (A_QBLOCK, LANES), 1) < HEAD_DIM

    def load(blk, lanes):
        t0 = pl.multiple_of(blk * A_QBLOCK, A_QBLOCK)
        start = pl.multiple_of(jnp.clip(t0 - A_HALF, 0, length - A_KWIN), A_HALF)
        case = jnp.where(blk == 0, 0, jnp.where(blk == nblk - 1, 2, 1))
        q = q_ref[0, pl.ds(t0, A_QBLOCK), lanes]
        q_heads = jnp.concatenate([jnp.where(lane_head == hd, q, jnp.zeros_like(q)) for hd in range(A_HEADS)],
                                  axis=0)
        return t0, case, q_heads, k_ref[0, pl.ds(start, A_KWIN), lanes], v_ref[0, pl.ds(start, A_KWIN), lanes]

    def softmax(s_all, case):
        ps, ms, ls = [], [], []
        for hd in range(A_HEADS):
            s = s_all[hd * A_QBLOCK:(hd + 1) * A_QBLOCK] + bias_ref[case, hd]
            m = jnp.max(s, axis=-1, keepdims=True)
            p = jnp.exp2(s - m)
            ls.append(jnp.sum(p, axis=-1, keepdims=True))
            ms.append(m)
            ps.append(p.astype(_BF16))
        return ps, ms, ls

    def values(t0, lanes, ps, ms, ls, vw):
        outs, lses = [], []
        for pair in range(A_HEADS // 2):
            h0, h1 = 2 * pair, 2 * pair + 1
            acc = _dot(jnp.concatenate([ps[h0], ps[h1]], axis=0), vw[:, pair * LANES:(pair + 1) * LANES])
            l_pair = jnp.where(low, ls[h0], ls[h1])
            outs.append(jnp.where(low, acc[:A_QBLOCK], acc[A_QBLOCK:]) * (1.0 / l_pair))
            lses.append(jnp.where(low, ms[h0], ms[h1]) + jnp.log2(l_pair))
        o_ref[0, pl.ds(t0, A_QBLOCK), lanes] = jnp.concatenate(outs, axis=-1).astype(o_ref.dtype)
        lse_ref[0, pl.ds(t0, A_QBLOCK), lanes] = jnp.concatenate(lses, axis=-1)

    def two_blocks(i, lanes):
        blocks = [load(2 * i + j, lanes) for j in range(2)]
        scores = [_dot_nt(q_heads, kw) for _, _, q_heads, kw, _ in blocks]
        for (t0, case, _, _, vw), s_all in zip(blocks, scores):
            values(t0, lanes, *softmax(s_all, case), vw)

    def residue_class(lanes):
        def body(i, carry):
            two_blocks(i, lanes)
            return carry
        lax.fori_loop(0, nblk // 2, body, 0, unroll=min(nblk // 2, A_LOOP_UNROLL))

    for g in range(group):
        residue_class(slice(g * A_WIDTH, (g + 1) * A_WIDTH))


def _dilated_pattern(q, k, v, dilation, slopes):
    B, length, _ = q.shape
    group = min(dilation, A_RESIDUES_PER_STEP)
    spec = pl.BlockSpec((1, length, group * A_WIDTH), lambda b, r: (b, 0, r))
    return pl.pallas_call(
        functools.partial(_dilated_kernel, length=length, dilation=dilation, group=group, slopes=slopes),
        out_shape=(jax.ShapeDtypeStruct(q.shape, _BF16), jax.ShapeDtypeStruct(q.shape, _F32)),
        grid=(B, dilation // group),
        in_specs=[spec, spec, spec],
        out_specs=(spec, spec),
        scratch_shapes=[pltpu.VMEM((3, A_HEADS, A_QBLOCK, A_KWIN), _F32)],
        compiler_params=pltpu.CompilerParams(dimension_semantics=("arbitrary", "arbitrary"),
                                             vmem_limit_bytes=VMEM_LIMIT),
        name=f"dilated_d{dilation}",
    )(q, k, v)


def _windowed_kernel(sink_ref, q_ref, k_ref, v_ref, *rest, seq, slopes):
    gate_refs, (o_ref, bias_ref) = rest[:B_HEADS // 2], rest[B_HEADS // 2:]
    nblk = seq // B_QBLOCK

    @pl.when((pl.program_id(0) == 0) & (pl.program_id(1) == 0))
    def _():
        rel0 = (lax.broadcasted_iota(jnp.int32, (B_QBLOCK, B_KWIN), 1)
                - lax.broadcasted_iota(jnp.int32, (B_QBLOCK, B_KWIN), 0))
        for case, shift in enumerate((0, -B_WINDOW, -2 * B_WINDOW)):
            dist = jnp.abs(rel0 + shift)
            for idx, head in enumerate(B_HEAD_ORDER):
                bias_ref[case, idx] = jnp.where(dist <= B_WINDOW, dist.astype(_F32) * (-slopes[head] * LOG2E),
                                                NEG_INF)

    low = lax.broadcasted_iota(jnp.int32, (B_QBLOCK, LANES), 1) < HEAD_DIM
    npair = B_HEADS // 2
    blocks = []
    for j in range(B_QBLOCKS_PER_STEP):
        blk = pl.program_id(1) * B_QBLOCKS_PER_STEP + j
        t0 = pl.multiple_of(blk * B_QBLOCK, B_QBLOCK)
        start = pl.multiple_of(jnp.clip(t0 - B_WINDOW, 0, seq - B_KWIN), B_WINDOW)
        case = jnp.where(blk == 0, 0, jnp.where(blk == nblk - 1, 2, 1))
        rows = slice(j * B_QBLOCK, (j + 1) * B_QBLOCK)
        blocks.append((rows, case, k_ref[0, pl.ds(start, B_KWIN), :], v_ref[0, pl.ds(start, B_KWIN), :]))

    def scores(item):
        (rows, _, kw, _), pair = item
        q = q_ref[0, rows, pair * LANES:(pair + 1) * LANES]
        zero = jnp.zeros_like(q)
        return _dot_nt(jnp.concatenate([jnp.where(low, q, zero), jnp.where(low, zero, q)], axis=0), kw)

    items = [(block, pair) for block in blocks for pair in range(npair)]
    s_next = scores(items[0])
    for n, ((rows, case, _, vw), pair) in enumerate(items):
        s_pair, s_next = s_next, (scores(items[n + 1]) if n + 1 < len(items) else None)
        ps, ms, ls, sinks = [], [], [], []
        for side in range(2):
            idx = 2 * pair + side
            s = s_pair[side * B_QBLOCK:(side + 1) * B_QBLOCK] + bias_ref[case, idx]
            m = jnp.max(s, axis=-1, keepdims=True)
            p = jnp.exp2(s - m)
            ls.append(jnp.sum(p, axis=-1, keepdims=True))
            ms.append(m)
            sinks.append(sink_ref[B_HEAD_ORDER[idx]] * LOG2E)
            ps.append(p.astype(_BF16))
        acc = _dot(jnp.concatenate(ps, axis=0), vw)
        factor = 1.0 / (jnp.where(low, ls[0], ls[1])
                        + jnp.exp2(jnp.where(low, sinks[0], sinks[1]) - jnp.where(low, ms[0], ms[1])))
        out = jnp.where(low, acc[:B_QBLOCK], acc[B_QBLOCK:]) * factor * gate_refs[pair][0, rows, :].astype(_F32)
        o_ref[0, rows, pair * LANES:(pair + 1) * LANES] = out.astype(o_ref.dtype)


def _windowed(qb, kb, vb, gates, sink, slopes):
    B, S, _ = qb.shape
    kv_spec = pl.BlockSpec((1, S, B_KV_WIDTH), lambda b, i, sink: (b, 0, 0))
    q_spec = pl.BlockSpec((1, B_QSTEP, B_WIDTH), lambda b, i, sink: (b, i, 0))
    gate_specs = [pl.BlockSpec((1, B_QSTEP, LANES), lambda b, i, sink, blk=A_WIDTH // LANES + pair: (b, i, blk))
                  for pair in range(B_HEADS // 2)]
    return pl.pallas_call(
        functools.partial(_windowed_kernel, seq=S, slopes=slopes),
        out_shape=jax.ShapeDtypeStruct((B, S, B_WIDTH), _BF16),
        grid_spec=pltpu.PrefetchScalarGridSpec(
            num_scalar_prefetch=1, grid=(B, S // B_QSTEP),
            in_specs=[q_spec, kv_spec, kv_spec] + gate_specs, out_specs=q_spec,
            scratch_shapes=[pltpu.VMEM((3, B_HEADS, B_QBLOCK, B_KWIN), _F32)]),
        compiler_params=pltpu.CompilerParams(dimension_semantics=("arbitrary", "arbitrary"),
                                             vmem_limit_bytes=VMEM_LIMIT),
        name="windowed_gqa",
    )(sink, qb, kb, vb, *[gates] * (B_HEADS // 2))


def _latent_kernel(q_ref, k_ref, vt_ref, g_ref, o_ref, *, seq):
    nkb = seq // C_KBLOCK
    heads = range(C_PAIR)

    for qb in range(C_QBLOCKS_PER_STEP):
        rows = slice(qb * C_QBLOCK, (qb + 1) * C_QBLOCK)
        qs = [q_ref[0, hd, rows, :] for hd in heads]

        def scores(hd, j):
            return _dot_nt(k_ref[0, hd, j * C_KBLOCK:(j + 1) * C_KBLOCK, :], qs[hd])

        ms = [jnp.full((1, C_QBLOCK), NEG_INF, _F32) for _ in heads]
        accs = [jnp.zeros((C_VROWS, C_QBLOCK), _F32) for _ in heads]
        s_next = [scores(hd, 0) for hd in heads]
        for j in range(nkb):
            s_cur, s_next = s_next, ([scores(hd, j + 1) for hd in heads] if j + 1 < nkb else None)
            for hd in heads:
                m_new = jnp.maximum(ms[hd], jnp.max(s_cur[hd], axis=0, keepdims=True))
                alpha = jnp.exp2(ms[hd] - m_new)
                p = jnp.exp2(s_cur[hd] - m_new).astype(_BF16)
                accs[hd] = alpha * accs[hd] + _dot(vt_ref[0, hd, j], p)
                ms[hd] = m_new
        out_t = jnp.concatenate([accs[hd][:C_V] * (1.0 / accs[hd][C_V:C_V + 1]) for hd in heads], axis=0)
        o_ref[0, rows, :] = (out_t.T * g_ref[0, rows, :].astype(_F32)).astype(o_ref.dtype)


def _latent(qc, kc, vt, gates):
    B, H, S, _ = qc.shape
    nkb = S // C_KBLOCK
    gate_block0 = (A_WIDTH + B_WIDTH) // LANES
    return pl.pallas_call(
        functools.partial(_latent_kernel, seq=S),
        out_shape=jax.ShapeDtypeStruct((B, S, C_WIDTH), _BF16),
        grid=(B, H // C_PAIR, S // C_QSTEP),
        in_specs=[pl.BlockSpec((1, C_PAIR, C_QSTEP, C_PAD), lambda b, g, i: (b, g, i, 0)),
                  pl.BlockSpec((1, C_PAIR, S, C_PAD), lambda b, g, i: (b, g, 0, 0)),
                  pl.BlockSpec((1, C_PAIR, nkb, C_VROWS, C_KBLOCK), lambda b, g, i: (b, g, 0, 0, 0)),
                  pl.BlockSpec((1, C_QSTEP, C_PAIR * C_V), lambda b, g, i: (b, i, gate_block0 + g))],
        out_specs=pl.BlockSpec((1, C_QSTEP, C_PAIR * C_V), lambda b, g, i: (b, i, g)),
        compiler_params=pltpu.CompilerParams(dimension_semantics=("arbitrary", "arbitrary", "arbitrary"),
                                             vmem_limit_bytes=VMEM_LIMIT),
        name="latent_attention",
    )(qc, kc, vt, gates)


def _out_kernel(x_ref, g_ref, o1_ref, o2_ref, o3_ref, l1_ref, l2_ref, l3_ref, yb_ref, yc_ref,
                w_o_ref, post_g_ref, out_ref, y_scr, pat_scr):
    outs, lses = [], []
    for p, ((_, d), o_ref, l_ref) in enumerate(zip(A_PATTERNS, (o1_ref, o2_ref, o3_ref), (l1_ref, l2_ref, l3_ref))):
        if d == 1:
            outs.append(o_ref[0].astype(_F32))
            lses.append(l_ref[0])
            continue
        rows = ROW_TILE // d
        nchunk = A_WIDTH // LANES
        for r in range(d):
            for c in range(nchunk):
                lanes = slice(r * A_WIDTH + c * LANES, r * A_WIDTH + (c + 1) * LANES)
                pat_scr[2 * p, c, pl.ds(r, rows, stride=d), :] = o_ref[0, :, lanes].astype(_F32)
                pat_scr[2 * p + 1, c, pl.ds(r, rows, stride=d), :] = l_ref[0, :, lanes]
        outs.append(jnp.concatenate([pat_scr[2 * p, c] for c in range(nchunk)], axis=-1))
        lses.append(jnp.concatenate([pat_scr[2 * p + 1, c] for c in range(nchunk)], axis=-1))
    l1, l2, l3 = lses
    big = jnp.maximum(jnp.maximum(l1, l2), l3)
    e1, e2, e3 = jnp.exp2(l1 - big), jnp.exp2(l2 - big), jnp.exp2(l3 - big)
    num = e1 * outs[0] + e2 * outs[1] + e3 * outs[2]
    ya = num / (e1 + e2 + e3)
    y_scr[:, :A_WIDTH] = (ya * g_ref[0].astype(_F32)).astype(_BF16)
    y_scr[:, A_WIDTH:A_WIDTH + B_WIDTH] = yb_ref[0]
    y_scr[:, A_WIDTH + B_WIDTH:] = yc_ref[0]
    y = _dot(y_scr[...], w_o_ref[0])
    out_ref[0] = x_ref[0] + _rms(y, post_g_ref[0])


def _out_proj(layer, x, gates, o_pats, lse_pats, yb, yc, w_o, post_g):
    B, S, _ = x.shape
    row = lambda n: pl.BlockSpec((1, ROW_TILE, n), lambda b, i: (b, i, 0))
    full = lambda a: _layer_block(a, layer)
    return pl.pallas_call(
        _out_kernel,
        out_shape=jax.ShapeDtypeStruct(x.shape, x.dtype),
        grid=(B, S // ROW_TILE),
        in_specs=[row(D_MODEL), row(A_WIDTH)] + [_residue_view_rows(d) for _, d in A_PATTERNS] * 2
                 + [row(B_WIDTH), row(C_WIDTH), full(w_o), full(post_g)],
        out_specs=row(D_MODEL),
        scratch_shapes=[pltpu.VMEM((ROW_TILE, D_MIX), _BF16),
                        pltpu.VMEM((2 * len(A_PATTERNS), A_WIDTH // LANES, ROW_TILE, LANES), _F32)],
        compiler_params=pltpu.CompilerParams(dimension_semantics=("arbitrary", "arbitrary"),
                                             vmem_limit_bytes=VMEM_LIMIT),
        name="out_proj",
    )(x, gates, *o_pats, *lse_pats, yb, yc, w_o, post_g)


def _pair_heads(t, axis):
    parts = jnp.split(t, B_HEADS, axis=axis)
    return jnp.concatenate([parts[h] for h in B_HEAD_ORDER], axis=axis)


def _rot_cols(w):
    half = C_ROPE // 2
    return jnp.concatenate([-w[..., half:], w[..., :half]], axis=-1)


def _stacked_weights(w_in, w_uq, w_ukv, w_o):
    depth = w_in.shape[0]
    o = 0
    cols = {}
    for name, n in (("qa", A_WIDTH), ("ka", A_WIDTH), ("va", A_WIDTH), ("ga", A_WIDTH),
                    ("qb", B_WIDTH), ("kb", B_KV_WIDTH), ("vb", B_KV_WIDTH), ("gb", B_WIDTH),
                    ("cq", Q_LORA), ("ckv", KV_LORA), ("kr", C_ROPE), ("gc", C_WIDTH)):
        cols[name] = w_in[..., o:o + n]
        o += n
    w_a = jnp.concatenate([cols["qa"], cols["ka"], cols["va"]], axis=-1)
    w_g = jnp.concatenate([cols["ga"], _pair_heads(cols["gb"], -1), cols["gc"]], axis=-1)
    w_b = jnp.concatenate([_pair_heads(cols["qb"], -1), cols["kb"], cols["vb"]], axis=-1)
    w_c = jnp.concatenate([cols["cq"], cols["ckv"], jnp.zeros((depth, D_MODEL, C_NOPE), w_in.dtype),
                           cols["kr"], _rot_cols(cols["kr"])], axis=-1)
    uq = w_uq.reshape(depth, Q_LORA, C_HEADS, C_NOPE + C_ROPE)
    uq = jnp.concatenate([uq, _rot_cols(uq[..., C_NOPE:])], axis=-1).reshape(depth, Q_LORA, C_HEADS * C_PAD)
    ukv = w_ukv.reshape(depth, KV_LORA, C_HEADS, C_NOPE + C_V)
    uk = jnp.concatenate([ukv[..., :C_NOPE], jnp.zeros((depth, KV_LORA, C_HEADS, C_PAD - C_NOPE), w_ukv.dtype)],
                         axis=-1).reshape(depth, KV_LORA, C_HEADS * C_PAD)
    uvt = jnp.transpose(ukv[..., C_NOPE:], (0, 2, 3, 1))
    uvt = jnp.concatenate([uvt, jnp.zeros((depth, C_HEADS, C_VROWS - C_V, KV_LORA), w_ukv.dtype)], axis=2)
    uvt = uvt.reshape(depth, C_HEADS * C_VROWS, KV_LORA)

    w_o_p = jnp.concatenate([w_o[:, :A_WIDTH], _pair_heads(w_o[:, A_WIDTH:A_WIDTH + B_WIDTH], 1),
                             w_o[:, A_WIDTH + B_WIDTH:]], axis=1)
    bf = lambda t: t.astype(_BF16)
    return (bf(w_a), bf(w_g), bf(w_b), bf(w_c), bf(uq), bf(uk), bf(uvt)), bf(w_o_p)


def kernel(x, positions, pre_norm, w_in, q_a_norm, kv_a_norm, w_uq, w_ukv, sink, w_o, post_norm):
    depth = w_in.shape[0]
    slopes_a, slopes_b = _alibi_slopes()
    cos_t, sin_t = _rope_tables(positions)
    in_w, w_o_p = _stacked_weights(w_in, w_uq, w_ukv, w_o)
    gains = [g[:, None, :] for g in (pre_norm, q_a_norm, kv_a_norm, post_norm)]
    for i in range(depth):
        qkv_a, (gates, qb, kb, vb, qc, kc, vt) = _in_proj(i, x, cos_t, sin_t, *gains[:3], *in_w)
        pats = [_dilated_pattern(*qkv, d, slopes_a) for qkv, (_, d) in zip(qkv_a, A_PATTERNS)]
        yb = _windowed(qb, kb, vb, gates, sink[i], slopes_b)
        yc = _latent(qc, kc, vt, gates)
        x = _out_proj(i, x, gates, [p[0] for p in pats], [p[1] for p in pats], yb, yc, w_o_p, gains[3])
    return x
```

```python
import functools
import math

import numpy as np
import jax
import jax.numpy as jnp
from jax import lax
from jax.experimental import pallas as pl
from jax.experimental.pallas import tpu as pltpu

D_MODEL = 1024
HEAD_DIM = 64
A_HEADS = 4
A_WIDTH = A_HEADS * HEAD_DIM
A_PATTERNS = ((128, 1), (512, 4), (2048, 16))
A_HALF = 64
B_HEADS = 6
B_KV_HEADS = 2
B_GROUP = B_HEADS // B_KV_HEADS
B_WIDTH = B_HEADS * HEAD_DIM
B_KV_WIDTH = B_KV_HEADS * HEAD_DIM
B_WINDOW = 128
C_HEADS = 6
C_NOPE = 64
C_ROPE = 32
C_V = 64
C_WIDTH = C_HEADS * C_V
Q_LORA = 256
KV_LORA = 128
ROPE_THETA = 10000.0
D_MIX = A_WIDTH + B_WIDTH + C_WIDTH
N_ALIBI = A_HEADS + B_HEADS
RMS_EPS = 1e-6
NEG_INF = -1e30
LOG2E = math.log2(math.e)

LANES = 128
C_PAD = LANES
C_VROWS = 80
C_PAIR = 2

ROW_TILE = 1024
A_QBLOCK = 128
A_KWIN = A_QBLOCK + 2 * A_HALF
A_RESIDUES_PER_STEP = 4
A_LOOP_UNROLL = 2
B_QBLOCK = 128
B_KWIN = B_QBLOCK + 2 * B_WINDOW
B_QBLOCKS_PER_STEP = 4
B_QSTEP = B_QBLOCKS_PER_STEP * B_QBLOCK
C_QBLOCK = 512
C_KBLOCK = 256
C_QBLOCKS_PER_STEP = 4
C_QSTEP = C_QBLOCKS_PER_STEP * C_QBLOCK
VMEM_LIMIT = 48 * 1024 * 1024

B_HEAD_ORDER = (0, 3, 1, 4, 2, 5)

_F32 = jnp.float32
_BF16 = jnp.bfloat16


def _alibi_slopes():
    s = 2.0 ** (-8.0 * np.arange(1, N_ALIBI + 1, dtype=np.float64) / N_ALIBI)
    return [float(v) for v in s[B_HEADS:]], [float(v) for v in s[:B_HEADS]]


def _rms(x, g):
    return x * lax.rsqrt(jnp.mean(x * x, axis=-1, keepdims=True) + RMS_EPS) * g


def _residue_view_shape(batch, seq, d, dtype):
    return jax.ShapeDtypeStruct((batch, seq // d, d * A_WIDTH), dtype)


def _layer_block(stacked, layer):
    return pl.BlockSpec((1,) + stacked.shape[1:], lambda b, i: (layer,) + (0,) * (stacked.ndim - 1))


def _residue_view_rows(d):
    return pl.BlockSpec((1, ROW_TILE // d, d * A_WIDTH), lambda b, i: (b, i, 0))


def _dot(a, b):
    return jnp.dot(a, b, preferred_element_type=_F32)


def _dot_nt(a, b):
    return lax.dot_general(a, b, (((1,), (1,)), ((), ())), preferred_element_type=_F32)


def _rope_table_kernel(pos_ref, freq_ref, cos_ref, sin_ref):
    ang = freq_ref[...] * pos_ref[0].astype(_F32)
    cos, sin = jnp.cos(ang), jnp.sin(ang)
    tokens = ang.shape[1]
    ones = jnp.ones((C_NOPE, tokens), _F32)
    zeros = jnp.zeros((C_NOPE, tokens), _F32)
    tail = zeros[:C_PAD - C_NOPE - C_ROPE]
    cos_ref[0] = jnp.concatenate([ones, cos, cos, tail], axis=0).T
    sin_ref[0] = jnp.concatenate([zeros, sin, sin, tail], axis=0).T


def _rope_tables(positions):
    B, S = positions.shape
    half = C_ROPE // 2
    freq = ROPE_THETA ** (-2.0 * jnp.arange(half, dtype=_F32) / C_ROPE)
    out = jax.ShapeDtypeStruct((B, S, C_PAD), _F32)
    return pl.pallas_call(
        _rope_table_kernel,
        out_shape=(out, out),
        grid=(B, S // ROW_TILE),
        in_specs=[pl.BlockSpec((1, 1, ROW_TILE), lambda b, i: (b, 0, i)),
                  pl.BlockSpec((half, 1), lambda b, i: (0, 0))],
        out_specs=(pl.BlockSpec((1, ROW_TILE, C_PAD), lambda b, i: (b, i, 0)),
                   pl.BlockSpec((1, ROW_TILE, C_PAD), lambda b, i: (b, i, 0))),
        compiler_params=pltpu.CompilerParams(dimension_semantics=("arbitrary", "arbitrary")),
        name="rope_tables",
    )(positions.reshape(B, 1, S), freq.reshape(half, 1))


def _in_proj_kernel(x_ref, cos_ref, sin_ref, pre_g_ref, qn_g_ref, kvn_g_ref,
                    w_a_ref, w_g_ref, w_b_ref, w_c_ref, w_uq_ref, w_uk_ref, w_uvt_ref,
                    *rest):
    a_refs, (g_ref, qb_ref, kb_ref, vb_ref, qc_ref, kc_ref, vt_ref, a_scr) = rest[:3 * len(A_PATTERNS)], rest[-8:]
    h = _rms(x_ref[0], pre_g_ref[0]).astype(_BF16)
    score_scale = HEAD_DIM ** -0.5 * LOG2E

    pa = _dot(h, w_a_ref[0])
    nchunk = 3 * A_WIDTH // LANES
    for c in range(nchunk):
        chunk = pa[:, c * LANES:(c + 1) * LANES]
        a_scr[c] = chunk * score_scale if c < A_WIDTH // LANES else chunk
    for p, (_, d) in enumerate(A_PATTERNS):
        rows = ROW_TILE // d
        for r in range(d):
            for c in range(nchunk):
                res = a_scr[c, pl.ds(r, rows, stride=d), :] if d > 1 else a_scr[c]
                lane0 = r * A_WIDTH + (c * LANES) % A_WIDTH
                a_refs[3 * p + c * LANES // A_WIDTH][0, :, lane0:lane0 + LANES] = res.astype(_BF16)

    for c in range(0, D_MIX, 256):
        g = _dot(h, w_g_ref[0, :, c:c + 256])
        g_ref[0, :, c:c + 256] = (g / (1.0 + jnp.exp(-g))).astype(_BF16)

    pb = _dot(h, w_b_ref[0])
    qb_ref[0] = (pb[:, :B_WIDTH] * score_scale).astype(_BF16)
    kb_ref[0] = pb[:, B_WIDTH:B_WIDTH + B_KV_WIDTH].astype(_BF16)
    vb_ref[0] = pb[:, B_WIDTH + B_KV_WIDTH:].astype(_BF16)

    pc = _dot(h, w_c_ref[0])
    cos = cos_ref[0]
    sin = sin_ref[0]

    def rotary(t):
        return t * cos + pltpu.roll(t, C_PAD - C_ROPE, axis=1) * sin

    cq = _rms(pc[:, :Q_LORA], qn_g_ref[0]).astype(_BF16)
    ckv = _rms(pc[:, Q_LORA:Q_LORA + KV_LORA], kvn_g_ref[0]).astype(_BF16)
    k_rope = rotary(pc[:, Q_LORA + KV_LORA:])
    c_scale = (C_NOPE + C_ROPE) ** -0.5 * LOG2E
    for hd in range(C_HEADS):
        cols = slice(hd * C_PAD, (hd + 1) * C_PAD)
        qc_ref[0, hd] = (rotary(_dot(cq, w_uq_ref[0, :, cols])) * c_scale).astype(_BF16)
        kc_ref[0, hd] = (_dot(ckv, w_uk_ref[0, :, cols]) + k_rope).astype(_BF16)
    vt = _dot_nt(w_uvt_ref[0], ckv)
    row = lax.broadcasted_iota(jnp.int32, vt.shape, 0)
    ones_row = functools.reduce(jnp.logical_or, [row == hd * C_VROWS + C_V for hd in range(C_HEADS)])
    vt = jnp.where(ones_row, 1.0, vt).astype(_BF16)
    for hd in range(C_HEADS):
        for c in range(ROW_TILE // C_KBLOCK):
            vt_ref[0, hd, c] = vt[hd * C_VROWS:(hd + 1) * C_VROWS, c * C_KBLOCK:(c + 1) * C_KBLOCK]


def _in_proj(layer, x, cos_t, sin_t, pre_g, qn_g, kvn_g, w_a, w_g, w_b, w_c, w_uq, w_uk, w_uvt):
    B, S, _ = x.shape
    nblk = S // ROW_TILE
    row = lambda n: pl.BlockSpec((1, ROW_TILE, n), lambda b, i: (b, i, 0))
    full = lambda a: _layer_block(a, layer)
    bf = lambda n: jax.ShapeDtypeStruct((B, S, n), _BF16)
    head4 = pl.BlockSpec((1, C_HEADS, ROW_TILE, C_PAD), lambda b, i: (b, 0, i, 0))
    weights = (pre_g, qn_g, kvn_g, w_a, w_g, w_b, w_c, w_uq, w_uk, w_uvt)
    a_shapes = tuple(_residue_view_shape(B, S, d, _BF16) for _, d in A_PATTERNS for _ in range(3))
    a_specs = tuple(_residue_view_rows(d) for _, d in A_PATTERNS for _ in range(3))
    outs = pl.pallas_call(
        _in_proj_kernel,
        out_shape=a_shapes + (bf(D_MIX), bf(B_WIDTH), bf(B_KV_WIDTH), bf(B_KV_WIDTH),
                              jax.ShapeDtypeStruct((B, C_HEADS, S, C_PAD), _BF16),
                              jax.ShapeDtypeStruct((B, C_HEADS, S, C_PAD), _BF16),
                              jax.ShapeDtypeStruct((B, C_HEADS, S // C_KBLOCK, C_VROWS, C_KBLOCK), _BF16)),
        grid=(B, nblk),
        in_specs=[row(D_MODEL), row(C_PAD), row(C_PAD)] + [full(w) for w in weights],
        out_specs=a_specs + (row(D_MIX), row(B_WIDTH), row(B_KV_WIDTH), row(B_KV_WIDTH), head4, head4,
                             pl.BlockSpec((1, C_HEADS, ROW_TILE // C_KBLOCK, C_VROWS, C_KBLOCK),
                                          lambda b, i: (b, 0, i, 0, 0))),
        scratch_shapes=[pltpu.VMEM((3 * A_WIDTH // LANES, ROW_TILE, LANES), _F32)],
        compiler_params=pltpu.CompilerParams(dimension_semantics=("arbitrary", "arbitrary"),
                                             vmem_limit_bytes=VMEM_LIMIT),
        name="in_proj",
    )(x, cos_t, sin_t, *weights)
    n_a = 3 * len(A_PATTERNS)
    return [outs[3 * p:3 * p + 3] for p in range(len(A_PATTERNS))], outs[n_a:]


def _dilated_kernel(q_ref, k_ref, v_ref, o_ref, lse_ref, bias_ref, *, length, dilation, group, slopes):
    nblk = length // A_QBLOCK
    lane_head = lax.shift_right_logical(lax.broadcasted_iota(jnp.int32, (A_QBLOCK, A_WIDTH), 1),
                                        int(math.log2(HEAD_DIM)))

    @pl.when((pl.program_id(0) == 0) & (pl.program_id(1) == 0))
    def _():
        rel0 = (lax.broadcasted_iota(jnp.int32, (A_QBLOCK, A_KWIN), 1)
                - lax.broadcasted_iota(jnp.int32, (A_QBLOCK, A_KWIN), 0))
        for case, shift in enumerate((0, -A_HALF, -2 * A_HALF)):
            dist = jnp.abs(rel0 + shift)
            for hd in range(A_HEADS):
                bias_ref[case, hd] = jnp.where(dist <= A_HALF,
                                               dist.astype(_F32) * (-slopes[hd] * dilation * LOG2E), NEG_INF)

    low = lax.broadcasted_iota(jnp.int32, (A_QBLOCK, LANES), 1) < HEAD_DIM

    def load(blk, lanes):
        t0 = pl.multiple_of(blk * A_QBLOCK, A_QBLOCK)
        start = pl.multiple_of(jnp.clip(t0 - A_HALF, 0, length - A_KWIN), A_HALF)
        case = jnp.where(blk == 0, 0, jnp.where(blk == nblk - 1, 2, 1))
        q = q_ref[0, pl.ds(t0, A_QBLOCK), lanes]
        q_heads = jnp.concatenate([jnp.where(lane_head == hd, q, jnp.zeros_like(q)) for hd in range(A_HEADS)],
                                  axis=0)
        return t0, case, q_heads, k_ref[0, pl.ds(start, A_KWIN), lanes], v_ref[0, pl.ds(start, A_KWIN), lanes]

    def softmax(s_all, case):
        ps, ms, ls = [], [], []
        for hd in range(A_HEADS):
            s = s_all[hd * A_QBLOCK:(hd + 1) * A_QBLOCK] + bias_ref[case, hd]
            m = jnp.max(s, axis=-1, keepdims=True)
            p = jnp.exp2(s - m)
            ls.append(jnp.sum(p, axis=-1, keepdims=True))
            ms.append(m)
            ps.append(p.astype(_BF16))
        return ps, ms, ls

    def values(t0, lanes, ps, ms, ls, vw):
        outs, lses = [], []
        for pair in range(A_HEADS // 2):
            h0, h1 = 2 * pair, 2 * pair + 1
            acc = _dot(jnp.concatenate([ps[h0], ps[h1]], axis=0), vw[:, pair * LANES:(pair + 1) * LANES])
            l_pair = jnp.where(low, ls[h0], ls[h1])
            outs.append(jnp.where(low, acc[:A_QBLOCK], acc[A_QBLOCK:]) * (1.0 / l_pair))
            lses.append(jnp.where(low, ms[h0], ms[h1]) + jnp.log2(l_pair))
        o_ref[0, pl.ds(t0, A_QBLOCK), lanes] = jnp.concatenate(outs, axis=-1).astype(o_ref.dtype)
        lse_ref[0, pl.ds(t0, A_QBLOCK), lanes] = jnp.concatenate(lses, axis=-1)

    def two_blocks(i, lanes):
        blocks = [load(2 * i + j, lanes) for j in range(2)]
        scores = [_dot_nt(q_heads, kw) for _, _, q_heads, kw, _ in blocks]
        for (t0, case, _, _, vw), s_all in zip(blocks, scores):
            values(t0, lanes, *softmax(s_all, case), vw)

    def residue_class(lanes):
        def body(i, carry):
            two_blocks(i, lanes)
            return carry
        lax.fori_loop(0, nblk // 2, body, 0, unroll=min(nblk // 2, A_LOOP_UNROLL))

    for g in range(group):
        residue_class(slice(g * A_WIDTH, (g + 1) * A_WIDTH))


def _dilated_pattern(q, k, v, dilation, slopes):
    B, length, _ = q.shape
    group = min(dilation, A_RESIDUES_PER_STEP)
    spec = pl.BlockSpec((1, length, group * A_WIDTH), lambda b, r: (b, 0, r))
    return pl.pallas_call(
        functools.partial(_dilated_kernel, length=length, dilation=dilation, group=group, slopes=slopes),
        out_shape=(jax.ShapeDtypeStruct(q.shape, _BF16), jax.ShapeDtypeStruct(q.shape, _F32)),
        grid=(B, dilation // group),
        in_specs=[spec, spec, spec],
        out_specs=(spec, spec),
        scratch_shapes=[pltpu.VMEM((3, A_HEADS, A_QBLOCK, A_KWIN), _F32)],
        compiler_params=pltpu.CompilerParams(dimension_semantics=("arbitrary", "arbitrary"),
                                             vmem_limit_bytes=VMEM_LIMIT),
        name=f"dilated_d{dilation}",
    )(q, k, v)


def _windowed_kernel(sink_ref, q_ref, k_ref, v_ref, *rest, seq, slopes):
    gate_refs, (o_ref, bias_ref) = rest[:B_HEADS // 2], rest[B_HEADS // 2:]
    nblk = seq // B_QBLOCK

    @pl.when((pl.program_id(0) == 0) & (pl.program_id(1) == 0))
    def _():
        rel0 = (lax.broadcasted_iota(jnp.int32, (B_QBLOCK, B_KWIN), 1)
                - lax.broadcasted_iota(jnp.int32, (B_QBLOCK, B_KWIN), 0))
        for case, shift in enumerate((0, -B_WINDOW, -2 * B_WINDOW)):
            dist = jnp.abs(rel0 + shift)
            for idx, head in enumerate(B_HEAD_ORDER):
                bias_ref[case, idx] = jnp.where(dist <= B_WINDOW, dist.astype(_F32) * (-slopes[head] * LOG2E),
                                                NEG_INF)

    low = lax.broadcasted_iota(jnp.int32, (B_QBLOCK, LANES), 1) < HEAD_DIM
    npair = B_HEADS // 2
    blocks = []
    for j in range(B_QBLOCKS_PER_STEP):
        blk = pl.program_id(1) * B_QBLOCKS_PER_STEP + j
        t0 = pl.multiple_of(blk * B_QBLOCK, B_QBLOCK)
        start = pl.multiple_of(jnp.clip(t0 - B_WINDOW, 0, seq - B_KWIN), B_WINDOW)
        case = jnp.where(blk == 0, 0, jnp.where(blk == nblk - 1, 2, 1))
        rows = slice(j * B_QBLOCK, (j + 1) * B_QBLOCK)
        blocks.append((rows, case, k_ref[0, pl.ds(start, B_KWIN), :], v_ref[0, pl.ds(start, B_KWIN), :]))

    def scores(item):
        (rows, _, kw, _), pair = item
        q = q_ref[0, rows, pair * LANES:(pair + 1) * LANES]
        zero = jnp.zeros_like(q)
        return _dot_nt(jnp.concatenate([jnp.where(low, q, zero), jnp.where(low, zero, q)], axis=0), kw)

    items = [(block, pair) for block in blocks for pair in range(npair)]
    s_next = scores(items[0])
    for n, ((rows, case, _, vw), pair) in enumerate(items):
        s_pair, s_next = s_next, (scores(items[n + 1]) if n + 1 < len(items) else None)
        ps, ms, ls, sinks = [], [], [], []
        for side in range(2):
            idx = 2 * pair + side
            s = s_pair[side * B_QBLOCK:(side + 1) * B_QBLOCK] + bias_ref[case, idx]
            m = jnp.max(s, axis=-1, keepdims=True)
            p = jnp.exp2(s - m)
            ls.append(jnp.sum(p, axis=-1, keepdims=True))
            ms.append(m)
            sinks.append(sink_ref[B_HEAD_ORDER[idx]] * LOG2E)
            ps.append(p.astype(_BF16))
        acc = _dot(jnp.concatenate(ps, axis=0), vw)
        factor = 1.0 / (jnp.where(low, ls[0], ls[1])
                        + jnp.exp2(jnp.where(low, sinks[0], sinks[1]) - jnp.where(low, ms[0], ms[1])))
        out = jnp.where(low, acc[:B_QBLOCK], acc[B_QBLOCK:]) * factor * gate_refs[pair][0, rows, :].astype(_F32)
        o_ref[0, rows, pair * LANES:(pair + 1) * LANES] = out.astype(o_ref.dtype)


def _windowed(qb, kb, vb, gates, sink, slopes):
    B, S, _ = qb.shape
    kv_spec = pl.BlockSpec((1, S, B_KV_WIDTH), lambda b, i, sink: (b, 0, 0))
    q_spec = pl.BlockSpec((1, B_QSTEP, B_WIDTH), lambda b, i, sink: (b, i, 0))
    gate_specs = [pl.BlockSpec((1, B_QSTEP, LANES), lambda b, i, sink, blk=A_WIDTH // LANES + pair: (b, i, blk))
                  for pair in range(B_HEADS // 2)]
    return pl.pallas_call(
        functools.partial(_windowed_kernel, seq=S, slopes=slopes),
        out_shape=jax.ShapeDtypeStruct((B, S, B_WIDTH), _BF16),
        grid_spec=pltpu.PrefetchScalarGridSpec(
            num_scalar_prefetch=1, grid=(B, S // B_QSTEP),
            in_specs=[q_spec, kv_spec, kv_spec] + gate_specs, out_specs=q_spec,
            scratch_shapes=[pltpu.VMEM((3, B_HEADS, B_QBLOCK, B_KWIN), _F32)]),
        compiler_params=pltpu.CompilerParams(dimension_semantics=("arbitrary", "arbitrary"),
                                             vmem_limit_bytes=VMEM_LIMIT),
        name="windowed_gqa",
    )(sink, qb, kb, vb, *[gates] * (B_HEADS // 2))


def _latent_kernel(q_ref, k_ref, vt_ref, o_ref, *, seq):
    nkb = seq // C_KBLOCK
    heads = range(C_PAIR)

    for qb in range(C_QBLOCKS_PER_STEP):
        rows = slice(qb * C_QBLOCK, (qb + 1) * C_QBLOCK)
        qs = [q_ref[0, hd, rows, :] for hd in heads]

        def scores(hd, j):
            return _dot_nt(k_ref[0, hd, j * C_KBLOCK:(j + 1) * C_KBLOCK, :], qs[hd])

        ms = [jnp.full((1, C_QBLOCK), NEG_INF, _F32) for _ in heads]
        accs = [jnp.zeros((C_VROWS, C_QBLOCK), _F32) for _ in heads]
        s_next = [scores(hd, 0) for hd in heads]
        for j in range(nkb):
            s_cur, s_next = s_next, ([scores(hd, j + 1) for hd in heads] if j + 1 < nkb else None)
            for hd in heads:
                m_new = jnp.maximum(ms[hd], jnp.max(s_cur[hd], axis=0, keepdims=True))
                alpha = jnp.exp2(ms[hd] - m_new)
                p = jnp.exp2(s_cur[hd] - m_new).astype(_BF16)
                accs[hd] = alpha * accs[hd] + _dot(vt_ref[0, hd, j], p)
                ms[hd] = m_new
        out_t = jnp.concatenate([accs[hd][:C_V] * (1.0 / accs[hd][C_V:C_V + 1]) for hd in heads], axis=0)
        o_ref[0, rows, :] = out_t.T.astype(o_ref.dtype)


def _latent(qc, kc, vt):
    B, H, S, _ = qc.shape
    nkb = S // C_KBLOCK
    return pl.pallas_call(
        functools.partial(_latent_kernel, seq=S),
        out_shape=jax.ShapeDtypeStruct((B, S, C_WIDTH), _BF16),
        grid=(B, H // C_PAIR, S // C_QSTEP),
        in_specs=[pl.BlockSpec((1, C_PAIR, C_QSTEP, C_PAD), lambda b, g, i: (b, g, i, 0)),
                  pl.BlockSpec((1, C_PAIR, S, C_PAD), lambda b, g, i: (b, g, 0, 0)),
                  pl.BlockSpec((1, C_PAIR, nkb, C_VROWS, C_KBLOCK), lambda b, g, i: (b, g, 0, 0, 0))],
        out_specs=pl.BlockSpec((1, C_QSTEP, C_PAIR * C_V), lambda b, g, i: (b, i, g)),
        compiler_params=pltpu.CompilerParams(dimension_semantics=("arbitrary", "arbitrary", "arbitrary"),
                                             vmem_limit_bytes=VMEM_LIMIT),
        name="latent_attention",
    )(qc, kc, vt)


def _out_kernel(x_ref, g_ref, gc_ref, o1_ref, o2_ref, o3_ref, l1_ref, l2_ref, l3_ref, yb_ref, yc_ref,
                w_o_ref, post_g_ref, out_ref, y_scr, pat_scr):
    outs, lses = [], []
    for p, ((_, d), o_ref, l_ref) in enumerate(zip(A_PATTERNS, (o1_ref, o2_ref, o3_ref), (l1_ref, l2_ref, l3_ref))):
        if d == 1:
            outs.append(o_ref[0].astype(_F32))
            lses.append(l_ref[0])
            continue
        rows = ROW_TILE // d
        nchunk = A_WIDTH // LANES
        for r in range(d):
            for c in range(nchunk):
                lanes = slice(r * A_WIDTH + c * LANES, r * A_WIDTH + (c + 1) * LANES)
                pat_scr[2 * p, c, pl.ds(r, rows, stride=d), :] = o_ref[0, :, lanes].astype(_F32)
                pat_scr[2 * p + 1, c, pl.ds(r, rows, stride=d), :] = l_ref[0, :, lanes]
        outs.append(jnp.concatenate([pat_scr[2 * p, c] for c in range(nchunk)], axis=-1))
        lses.append(jnp.concatenate([pat_scr[2 * p + 1, c] for c in range(nchunk)], axis=-1))
    l1, l2, l3 = lses
    big = jnp.maximum(jnp.maximum(l1, l2), l3)
    e1, e2, e3 = jnp.exp2(l1 - big), jnp.exp2(l2 - big), jnp.exp2(l3 - big)
    num = e1 * outs[0] + e2 * outs[1] + e3 * outs[2]
    ya = num / (e1 + e2 + e3)
    y_scr[:, :A_WIDTH] = (ya * g_ref[0].astype(_F32)).astype(_BF16)
    y_scr[:, A_WIDTH:A_WIDTH + B_WIDTH] = yb_ref[0]
    y_scr[:, A_WIDTH + B_WIDTH:] = yc_ref[0] * gc_ref[0, :, D_MIX // 2 - C_WIDTH:]
    y = _dot(y_scr[...], w_o_ref[0])
    out_ref[0] = x_ref[0] + _rms(y, post_g_ref[0])


def _out_proj(layer, x, gates, o_pats, lse_pats, yb, yc, w_o, post_g):
    B, S, _ = x.shape
    row = lambda n: pl.BlockSpec((1, ROW_TILE, n), lambda b, i: (b, i, 0))
    full = lambda a: _layer_block(a, layer)
    return pl.pallas_call(
        _out_kernel,
        out_shape=jax.ShapeDtypeStruct(x.shape, x.dtype),
        grid=(B, S // ROW_TILE),
        in_specs=[row(D_MODEL), row(A_WIDTH), pl.BlockSpec((1, ROW_TILE, D_MIX // 2), lambda b, i: (b, i, 1))]
                 + [_residue_view_rows(d) for _, d in A_PATTERNS] * 2
                 + [row(B_WIDTH), row(C_WIDTH), full(w_o), full(post_g)],
        out_specs=row(D_MODEL),
        scratch_shapes=[pltpu.VMEM((ROW_TILE, D_MIX), _BF16),
                        pltpu.VMEM((2 * len(A_PATTERNS), A_WIDTH // LANES, ROW_TILE, LANES), _F32)],
        compiler_params=pltpu.CompilerParams(dimension_semantics=("arbitrary", "arbitrary"),
                                             vmem_limit_bytes=VMEM_LIMIT),
        name="out_proj",
    )(x, gates, gates, *o_pats, *lse_pats, yb, yc, w_o, post_g)


def _pair_heads(t, axis):
    parts = jnp.split(t, B_HEADS, axis=axis)
    return jnp.concatenate([parts[h] for h in B_HEAD_ORDER], axis=axis)


def _rot_cols(w):
    half = C_ROPE // 2
    return jnp.concatenate([-w[..., half:], w[..., :half]], axis=-1)


def _stacked_weights(w_in, w_uq, w_ukv, w_o):
    depth = w_in.shape[0]
    o = 0
    cols = {}
    for name, n in (("qa", A_WIDTH), ("ka", A_WIDTH), ("va", A_WIDTH), ("ga", A_WIDTH),
                    ("qb", B_WIDTH), ("kb", B_KV_WIDTH), ("vb", B_KV_WIDTH), ("gb", B_WIDTH),
                    ("cq", Q_LORA), ("ckv", KV_LORA), ("kr", C_ROPE), ("gc", C_WIDTH)):
        cols[name] = w_in[..., o:o + n]
        o += n
    w_a = jnp.concatenate([cols["qa"], cols["ka"], cols["va"]], axis=-1)
    w_g = jnp.concatenate([cols["ga"], _pair_heads(cols["gb"], -1), cols["gc"]], axis=-1)
    w_b = jnp.concatenate([_pair_heads(cols["qb"], -1), cols["kb"], cols["vb"]], axis=-1)
    w_c = jnp.concatenate([cols["cq"], cols["ckv"], jnp.zeros((depth, D_MODEL, C_NOPE), w_in.dtype),
                           cols["kr"], _rot_cols(cols["kr"])], axis=-1)
    uq = w_uq.reshape(depth, Q_LORA, C_HEADS, C_NOPE + C_ROPE)
    uq = jnp.concatenate([uq, _rot_cols(uq[..., C_NOPE:])], axis=-1).reshape(depth, Q_LORA, C_HEADS * C_PAD)
    ukv = w_ukv.reshape(depth, KV_LORA, C_HEADS, C_NOPE + C_V)
    uk = jnp.concatenate([ukv[..., :C_NOPE], jnp.zeros((depth, KV_LORA, C_HEADS, C_PAD - C_NOPE), w_ukv.dtype)],
                         axis=-1).reshape(depth, KV_LORA, C_HEADS * C_PAD)
    uvt = jnp.transpose(ukv[..., C_NOPE:], (0, 2, 3, 1))
    uvt = jnp.concatenate([uvt, jnp.zeros((depth, C_HEADS, C_VROWS - C_V, KV_LORA), w_ukv.dtype)], axis=2)
    uvt = uvt.reshape(depth, C_HEADS * C_VROWS, KV_LORA)

    w_o_p = jnp.concatenate([w_o[:, :A_WIDTH], _pair_heads(w_o[:, A_WIDTH:A_WIDTH + B_WIDTH], 1),
                             w_o[:, A_WIDTH + B_WIDTH:]], axis=1)
    bf = lambda t: t.astype(_BF16)
    return (bf(w_a), bf(w_g), bf(w_b), bf(w_c), bf(uq), bf(uk), bf(uvt)), bf(w_o_p)


def kernel(x, positions, pre_norm, w_in, q_a_norm, kv_a_norm, w_uq, w_ukv, sink, w_o, post_norm):
    depth = w_in.shape[0]
    slopes_a, slopes_b = _alibi_slopes()
    cos_t, sin_t = _rope_tables(positions)
    in_w, w_o_p = _stacked_weights(w_in, w_uq, w_ukv, w_o)
    gains = [g[:, None, :] for g in (pre_norm, q_a_norm, kv_a_norm, post_norm)]
    for i in range(depth):
        qkv_a, (gates, qb, kb, vb, qc, kc, vt) = _in_proj(i, x, cos_t, sin_t, *gains[:3], *in_w)
        pats = [_dilated_pattern(*qkv, d, slopes_a) for qkv, (_, d) in zip(qkv_a, A_PATTERNS)]
        yb = _windowed(qb, kb, vb, gates, sink[i], slopes_b)
        yc = _latent(qc, kc, vt)
        x = _out_proj(i, x, gates, [p[0] for p in pats], [p[1] for p in pats], yb, yc, w_o_p, gains[3])
    return x
```

```python
import functools
import math

import numpy as np
import jax
import jax.numpy as jnp
from jax import lax
from jax.experimental import pallas as pl
from jax.experimental.pallas import tpu as pltpu

D_MODEL = 1024
HEAD_DIM = 64
A_HEADS = 4
A_WIDTH = A_HEADS * HEAD_DIM
A_PATTERNS = ((128, 1), (512, 4), (2048, 16))
A_HALF = 64
B_HEADS = 6
B_KV_HEADS = 2
B_GROUP = B_HEADS // B_KV_HEADS
B_WIDTH = B_HEADS * HEAD_DIM
B_KV_WIDTH = B_KV_HEADS * HEAD_DIM
B_WINDOW = 128
C_HEADS = 6
C_NOPE = 64
C_ROPE = 32
C_V = 64
C_WIDTH = C_HEADS * C_V
Q_LORA = 256
KV_LORA = 128
ROPE_THETA = 10000.0
D_MIX = A_WIDTH + B_WIDTH + C_WIDTH
N_ALIBI = A_HEADS + B_HEADS
RMS_EPS = 1e-6
NEG_INF = -1e30
LOG2E = math.log2(math.e)

LANES = 128
C_PAD = LANES
C_VROWS = 80
C_PAIR = 2

ROW_TILE = 1024
A_QBLOCK = 128
A_KWIN = A_QBLOCK + 2 * A_HALF
A_RESIDUES_PER_STEP = 4
A_LOOP_UNROLL = 4
B_QBLOCK = 128
B_KWIN = B_QBLOCK + 2 * B_WINDOW
B_QBLOCKS_PER_STEP = 8
B_QSTEP = B_QBLOCKS_PER_STEP * B_QBLOCK
C_QBLOCK = 512
C_KBLOCK = 256
C_QBLOCKS_PER_STEP = 4
C_SCORE_LOOKAHEAD = 1
C_QSTEP = C_QBLOCKS_PER_STEP * C_QBLOCK
VMEM_LIMIT = 48 * 1024 * 1024

B_HEAD_ORDER = (0, 3, 1, 4, 2, 5)

_F32 = jnp.float32
_BF16 = jnp.bfloat16


def _alibi_slopes():
    s = 2.0 ** (-8.0 * np.arange(1, N_ALIBI + 1, dtype=np.float64) / N_ALIBI)
    return [float(v) for v in s[B_HEADS:]], [float(v) for v in s[:B_HEADS]]


def _rms(x, g):
    return x * lax.rsqrt(jnp.mean(x * x, axis=-1, keepdims=True) + RMS_EPS) * g


def _residue_view_shape(batch, seq, d, dtype):
    return jax.ShapeDtypeStruct((batch, seq // d, d * A_WIDTH), dtype)


def _layer_block(stacked, layer):
    return pl.BlockSpec((1,) + stacked.shape[1:], lambda b, i: (layer,) + (0,) * (stacked.ndim - 1))


def _residue_view_rows(d):
    return pl.BlockSpec((1, ROW_TILE // d, d * A_WIDTH), lambda b, i: (b, i, 0))


def _dot(a, b):
    return jnp.dot(a, b, preferred_element_type=_F32)


def _dot_nt(a, b):
    return lax.dot_general(a, b, (((1,), (1,)), ((), ())), preferred_element_type=_F32)


def _rope_table_kernel(pos_ref, freq_ref, cos_ref, sin_ref):
    ang = freq_ref[...] * pos_ref[0].astype(_F32)
    cos, sin = jnp.cos(ang), jnp.sin(ang)
    tokens = ang.shape[1]
    ones = jnp.ones((C_NOPE, tokens), _F32)
    zeros = jnp.zeros((C_NOPE, tokens), _F32)
    tail = zeros[:C_PAD - C_NOPE - C_ROPE]
    cos_ref[0] = jnp.concatenate([ones, cos, cos, tail], axis=0).T
    sin_ref[0] = jnp.concatenate([zeros, sin, sin, tail], axis=0).T


def _rope_tables(positions):
    B, S = positions.shape
    half = C_ROPE // 2
    freq = ROPE_THETA ** (-2.0 * jnp.arange(half, dtype=_F32) / C_ROPE)
    out = jax.ShapeDtypeStruct((B, S, C_PAD), _F32)
    return pl.pallas_call(
        _rope_table_kernel,
        out_shape=(out, out),
        grid=(B, S // ROW_TILE),
        in_specs=[pl.BlockSpec((1, 1, ROW_TILE), lambda b, i: (b, 0, i)),
                  pl.BlockSpec((half, 1), lambda b, i: (0, 0))],
        out_specs=(pl.BlockSpec((1, ROW_TILE, C_PAD), lambda b, i: (b, i, 0)),
                   pl.BlockSpec((1, ROW_TILE, C_PAD), lambda b, i: (b, i, 0))),
        compiler_params=pltpu.CompilerParams(dimension_semantics=("arbitrary", "arbitrary")),
        name="rope_tables",
    )(positions.reshape(B, 1, S), freq.reshape(half, 1))


def _in_proj_kernel(x_ref, cos_ref, sin_ref, pre_g_ref, qn_g_ref, kvn_g_ref,
                    w_a_ref, w_g_ref, w_b_ref, w_c_ref, w_uq_ref, w_uk_ref, w_uvt_ref,
                    *rest):
    a_refs, (g_ref, qb_ref, kb_ref, vb_ref, qc_ref, kc_ref, vt_ref, a_scr) = rest[:3 * len(A_PATTERNS)], rest[-8:]
    h = _rms(x_ref[0], pre_g_ref[0]).astype(_BF16)
    score_scale = HEAD_DIM ** -0.5 * LOG2E

    pa = _dot(h, w_a_ref[0])
    nchunk = 3 * A_WIDTH // LANES
    for c in range(nchunk):
        chunk = pa[:, c * LANES:(c + 1) * LANES]
        a_scr[c] = chunk * score_scale if c < A_WIDTH // LANES else chunk
    for p, (_, d) in enumerate(A_PATTERNS):
        rows = ROW_TILE // d
        for r in range(d):
            for c in range(nchunk):
                res = a_scr[c, pl.ds(r, rows, stride=d), :] if d > 1 else a_scr[c]
                lane0 = r * A_WIDTH + (c * LANES) % A_WIDTH
                a_refs[3 * p + c * LANES // A_WIDTH][0, :, lane0:lane0 + LANES] = res.astype(_BF16)

    for c in range(0, D_MIX, 256):
        g = _dot(h, w_g_ref[0, :, c:c + 256])
        g_ref[0, :, c:c + 256] = (g / (1.0 + jnp.exp(-g))).astype(_BF16)

    pb = _dot(h, w_b_ref[0])
    qb_ref[0] = (pb[:, :B_WIDTH] * score_scale).astype(_BF16)
    kb_ref[0] = pb[:, B_WIDTH:B_WIDTH + B_KV_WIDTH].astype(_BF16)
    vb_ref[0] = pb[:, B_WIDTH + B_KV_WIDTH:].astype(_BF16)

    pc = _dot(h, w_c_ref[0])
    cos = cos_ref[0]
    sin = sin_ref[0]

    def rotary(t):
        return t * cos + pltpu.roll(t, C_PAD - C_ROPE, axis=1) * sin

    cq = _rms(pc[:, :Q_LORA], qn_g_ref[0]).astype(_BF16)
    ckv = _rms(pc[:, Q_LORA:Q_LORA + KV_LORA], kvn_g_ref[0]).astype(_BF16)
    k_rope = rotary(pc[:, Q_LORA + KV_LORA:])
    c_scale = (C_NOPE + C_ROPE) ** -0.5 * LOG2E
    for pair in range(C_HEADS // 2):
        cols = slice(2 * pair * C_PAD, 2 * (pair + 1) * C_PAD)
        q_pair = _dot(cq, w_uq_ref[0, :, cols])
        k_pair = _dot(ckv, w_uk_ref[0, :, cols])
        for side in range(2):
            lanes = slice(side * C_PAD, (side + 1) * C_PAD)
            qc_ref[0, 2 * pair + side] = (rotary(q_pair[:, lanes]) * c_scale).astype(_BF16)
            kc_ref[0, 2 * pair + side] = (k_pair[:, lanes] + k_rope).astype(_BF16)
    vt = _dot_nt(w_uvt_ref[0], ckv)
    row = lax.broadcasted_iota(jnp.int32, vt.shape, 0)
    ones_row = functools.reduce(jnp.logical_or, [row == hd * C_VROWS + C_V for hd in range(C_HEADS)])
    vt = jnp.where(ones_row, 1.0, vt).astype(_BF16)
    for hd in range(C_HEADS):
        for c in range(ROW_TILE // C_KBLOCK):
            vt_ref[0, hd, c] = vt[hd * C_VROWS:(hd + 1) * C_VROWS, c * C_KBLOCK:(c + 1) * C_KBLOCK]


def _in_proj(layer, x, cos_t, sin_t, pre_g, qn_g, kvn_g, w_a, w_g, w_b, w_c, w_uq, w_uk, w_uvt):
    B, S, _ = x.shape
    nblk = S // ROW_TILE
    row = lambda n: pl.BlockSpec((1, ROW_TILE, n), lambda b, i: (b, i, 0))
    full = lambda a: _layer_block(a, layer)
    bf = lambda n: jax.ShapeDtypeStruct((B, S, n), _BF16)
    head4 = pl.BlockSpec((1, C_HEADS, ROW_TILE, C_PAD), lambda b, i: (b, 0, i, 0))
    weights = (pre_g, qn_g, kvn_g, w_a, w_g, w_b, w_c, w_uq, w_uk, w_uvt)
    a_shapes = tuple(_residue_view_shape(B, S, d, _BF16) for _, d in A_PATTERNS for _ in range(3))
    a_specs = tuple(_residue_view_rows(d) for _, d in A_PATTERNS for _ in range(3))
    outs = pl.pallas_call(
        _in_proj_kernel,
        out_shape=a_shapes + (bf(D_MIX), bf(B_WIDTH), bf(B_KV_WIDTH), bf(B_KV_WIDTH),
                              jax.ShapeDtypeStruct((B, C_HEADS, S, C_PAD), _BF16),
                              jax.ShapeDtypeStruct((B, C_HEADS, S, C_PAD), _BF16),
                              jax.ShapeDtypeStruct((B, C_HEADS, S // C_KBLOCK, C_VROWS, C_KBLOCK), _BF16)),
        grid=(B, nblk),
        in_specs=[row(D_MODEL), row(C_PAD), row(C_PAD)] + [full(w) for w in weights],
        out_specs=a_specs + (row(D_MIX), row(B_WIDTH), row(B_KV_WIDTH), row(B_KV_WIDTH), head4, head4,
                             pl.BlockSpec((1, C_HEADS, ROW_TILE // C_KBLOCK, C_VROWS, C_KBLOCK),
                                          lambda b, i: (b, 0, i, 0, 0))),
        scratch_shapes=[pltpu.VMEM((3 * A_WIDTH // LANES, ROW_TILE, LANES), _F32)],
        compiler_params=pltpu.CompilerParams(dimension_semantics=("arbitrary", "arbitrary"),
                                             vmem_limit_bytes=VMEM_LIMIT),
        name="in_proj",
    )(x, cos_t, sin_t, *weights)
    n_a = 3 * len(A_PATTERNS)
    return [outs[3 * p:3 * p + 3] for p in range(len(A_PATTERNS))], outs[n_a:]


def _dilated_kernel(q_ref, k_ref, v_ref, o_ref, lse_ref, bias_ref, *, length, dilation, group, slopes):
    nblk = length // A_QBLOCK
    lane_head = lax.shift_right_logical(lax.broadcasted_iota(jnp.int32, (A_QBLOCK, A_WIDTH), 1),
                                        int(math.log2(HEAD_DIM)))

    @pl.when((pl.program_id(0) == 0) & (pl.program_id(1) == 0))
    def _():
        rel0 = (lax.broadcasted_iota(jnp.int32, (A_QBLOCK, A_KWIN), 1)
                - lax.broadcasted_iota(jnp.int32, (A_QBLOCK, A_KWIN), 0))
        for case, shift in enumerate((0, -A_HALF, -2 * A_HALF)):
            dist = jnp.abs(rel0 + shift)
            for hd in range(A_HEADS):
                bias_ref[case, hd] = jnp.where(dist <= A_HALF,
                                               dist.astype(_F32) * (-slopes[hd] * dilation * LOG2E), NEG_INF)

    low = lax.broadcasted_iota(jnp.int32, (A_QBLOCK, LANES), 1) < HEAD_DIM

    def load(blk, lanes):
        t0 = pl.multiple_of(blk * A_QBLOCK, A_QBLOCK)
        start = pl.multiple_of(jnp.clip(t0 - A_HALF, 0, length - A_KWIN), A_HALF)
        case = jnp.where(blk == 0, 0, jnp.where(blk == nblk - 1, 2, 1))
        q = q_ref[0, pl.ds(t0, A_QBLOCK), lanes]
        q_heads = jnp.concatenate([jnp.where(lane_head == hd, q, jnp.zeros_like(q)) for hd in range(A_HEADS)],
                                  axis=0)
        return t0, case, q_heads, k_ref[0, pl.ds(start, A_KWIN), lanes], v_ref[0, pl.ds(start, A_KWIN), lanes]

    def softmax(s_all, case):
        ps, ms, ls = [], [], []
        for hd in range(A_HEADS):
            s = s_all[hd * A_QBLOCK:(hd + 1) * A_QBLOCK] + bias_ref[case, hd]
            m = jnp.max(s, axis=-1, keepdims=True)
            p = jnp.exp2(s - m)
            ls.append(jnp.sum(p, axis=-1, keepdims=True))
            ms.append(m)
            ps.append(p.astype(_BF16))
        return ps, ms, ls

    def values(t0, lanes, ps, ms, ls, vw):
        outs, lses = [], []
        for pair in range(A_HEADS // 2):
            h0, h1 = 2 * pair, 2 * pair + 1
            acc = _dot(jnp.concatenate([ps[h0], ps[h1]], axis=0), vw[:, pair * LANES:(pair + 1) * LANES])
            l_pair = jnp.where(low, ls[h0], ls[h1])
            outs.append(jnp.where(low, acc[:A_QBLOCK], acc[A_QBLOCK:]) * (1.0 / l_pair))
            lses.append(jnp.where(low, ms[h0], ms[h1]) + jnp.log2(l_pair))
        o_ref[0, pl.ds(t0, A_QBLOCK), lanes] = jnp.concatenate(outs, axis=-1).astype(o_ref.dtype)
        lse_ref[0, pl.ds(t0, A_QBLOCK), lanes] = jnp.concatenate(lses, axis=-1)

    def two_blocks(i, lanes):
        blocks = [load(2 * i + j, lanes) for j in range(2)]
        scores = [_dot_nt(q_heads, kw) for _, _, q_heads, kw, _ in blocks]
        for (t0, case, _, _, vw), s_all in zip(blocks, scores):
            values(t0, lanes, *softmax(s_all, case), vw)

    def residue_class(lanes):
        def body(i, carry):
            two_blocks(i, lanes)
            return carry
        lax.fori_loop(0, nblk // 2, body, 0, unroll=min(nblk // 2, A_LOOP_UNROLL))

    for g in range(group):
        residue_class(slice(g * A_WIDTH, (g + 1) * A_WIDTH))


def _dilated_pattern(q, k, v, dilation, slopes):
    B, length, _ = q.shape
    group = min(dilation, A_RESIDUES_PER_STEP)
    spec = pl.BlockSpec((1, length, group * A_WIDTH), lambda b, r: (b, 0, r))
    return pl.pallas_call(
        functools.partial(_dilated_kernel, length=length, dilation=dilation, group=group, slopes=slopes),
        out_shape=(jax.ShapeDtypeStruct(q.shape, _BF16), jax.ShapeDtypeStruct(q.shape, _F32)),
        grid=(B, dilation // group),
        in_specs=[spec, spec, spec],
        out_specs=(spec, spec),
        scratch_shapes=[pltpu.VMEM((3, A_HEADS, A_QBLOCK, A_KWIN), _F32)],
        compiler_params=pltpu.CompilerParams(dimension_semantics=("arbitrary", "arbitrary"),
                                             vmem_limit_bytes=VMEM_LIMIT),
        name=f"dilated_d{dilation}",
    )(q, k, v)


def _windowed_kernel(sink_ref, q_ref, k_ref, v_ref, *rest, seq, slopes):
    gate_refs, (o_ref, bias_ref) = rest[:B_HEADS // 2], rest[B_HEADS // 2:]
    nblk = seq // B_QBLOCK

    @pl.when((pl.program_id(0) == 0) & (pl.program_id(1) == 0))
    def _():
        rel0 = (lax.broadcasted_iota(jnp.int32, (B_QBLOCK, B_KWIN), 1)
                - lax.broadcasted_iota(jnp.int32, (B_QBLOCK, B_KWIN), 0))
        for case, shift in enumerate((0, -B_WINDOW, -2 * B_WINDOW)):
            dist = jnp.abs(rel0 + shift)
            for idx, head in enumerate(B_HEAD_ORDER):
                bias_ref[case, idx] = jnp.where(dist <= B_WINDOW, dist.astype(_F32) * (-slopes[head] * LOG2E),
                                                NEG_INF)

    low = lax.broadcasted_iota(jnp.int32, (B_QBLOCK, LANES), 1) < HEAD_DIM
    npair = B_HEADS // 2
    blocks = []
    for j in range(B_QBLOCKS_PER_STEP):
        blk = pl.program_id(1) * B_QBLOCKS_PER_STEP + j
        t0 = pl.multiple_of(blk * B_QBLOCK, B_QBLOCK)
        start = pl.multiple_of(jnp.clip(t0 - B_WINDOW, 0, seq - B_KWIN), B_WINDOW)
        case = jnp.where(blk == 0, 0, jnp.where(blk == nblk - 1, 2, 1))
        rows = slice(j * B_QBLOCK, (j + 1) * B_QBLOCK)
        blocks.append((rows, case, k_ref[0, pl.ds(start, B_KWIN), :], v_ref[0, pl.ds(start, B_KWIN), :]))

    def scores(item):
        (rows, _, kw, _), pair = item
        q = q_ref[0, rows, pair * LANES:(pair + 1) * LANES]
        zero = jnp.zeros_like(q)
        return _dot_nt(jnp.concatenate([jnp.where(low, q, zero), jnp.where(low, zero, q)], axis=0), kw)

    items = [(block, pair) for block in blocks for pair in range(npair)]
    s_next = scores(items[0])
    for n, ((rows, case, _, vw), pair) in enumerate(items):
        s_pair, s_next = s_next, (scores(items[n + 1]) if n + 1 < len(items) else None)
        ps, ms, ls, sinks = [], [], [], []
        for side in range(2):
            idx = 2 * pair + side
            s = s_pair[side * B_QBLOCK:(side + 1) * B_QBLOCK] + bias_ref[case, idx]
            m = jnp.max(s, axis=-1, keepdims=True)
            p = jnp.exp2(s - m)
            ls.append(jnp.sum(p, axis=-1, keepdims=True))
            ms.append(m)
            sinks.append(sink_ref[B_HEAD_ORDER[idx]] * LOG2E)
            ps.append(p.astype(_BF16))
        acc = _dot(jnp.concatenate(ps, axis=0), vw)
        factor = 1.0 / (jnp.where(low, ls[0], ls[1])
                        + jnp.exp2(jnp.where(low, sinks[0], sinks[1]) - jnp.where(low, ms[0], ms[1])))
        out = jnp.where(low, acc[:B_QBLOCK], acc[B_QBLOCK:]) * factor * gate_refs[pair][0, rows, :].astype(_F32)
        o_ref[0, rows, pair * LANES:(pair + 1) * LANES] = out.astype(o_ref.dtype)


def _windowed(qb, kb, vb, gates, sink, slopes):
    B, S, _ = qb.shape
    kv_spec = pl.BlockSpec((1, S, B_KV_WIDTH), lambda b, i, sink: (b, 0, 0))
    q_spec = pl.BlockSpec((1, B_QSTEP, B_WIDTH), lambda b, i, sink: (b, i, 0))
    gate_specs = [pl.BlockSpec((1, B_QSTEP, LANES), lambda b, i, sink, blk=A_WIDTH // LANES + pair: (b, i, blk))
                  for pair in range(B_HEADS // 2)]
    return pl.pallas_call(
        functools.partial(_windowed_kernel, seq=S, slopes=slopes),
        out_shape=jax.ShapeDtypeStruct((B, S, B_WIDTH), _BF16),
        grid_spec=pltpu.PrefetchScalarGridSpec(
            num_scalar_prefetch=1, grid=(B, S // B_QSTEP),
            in_specs=[q_spec, kv_spec, kv_spec] + gate_specs, out_specs=q_spec,
            scratch_shapes=[pltpu.VMEM((3, B_HEADS, B_QBLOCK, B_KWIN), _F32)]),
        compiler_params=pltpu.CompilerParams(dimension_semantics=("arbitrary", "arbitrary"),
                                             vmem_limit_bytes=VMEM_LIMIT),
        name="windowed_gqa",
    )(sink, qb, kb, vb, *[gates] * (B_HEADS // 2))


def _latent_kernel(q_ref, k_ref, vt_ref, o_ref, *, seq):
    nkb = seq // C_KBLOCK
    heads = range(C_PAIR)

    for qb in range(C_QBLOCKS_PER_STEP):
        rows = slice(qb * C_QBLOCK, (qb + 1) * C_QBLOCK)
        qs = [q_ref[0, hd, rows, :] for hd in heads]

        def scores(hd, j):
            return _dot_nt(k_ref[0, hd, j * C_KBLOCK:(j + 1) * C_KBLOCK, :], qs[hd])

        ms = [jnp.full((1, C_QBLOCK), NEG_INF, _F32) for _ in heads]
        accs = [jnp.zeros((C_VROWS, C_QBLOCK), _F32) for _ in heads]
        pending = [[scores(hd, j) for hd in heads] for j in range(C_SCORE_LOOKAHEAD)]
        for j in range(nkb):
            if j + C_SCORE_LOOKAHEAD < nkb:
                pending.append([scores(hd, j + C_SCORE_LOOKAHEAD) for hd in heads])
            s_cur = pending.pop(0)
            for hd in heads:
                m_new = jnp.maximum(ms[hd], jnp.max(s_cur[hd], axis=0, keepdims=True))
                alpha = jnp.exp2(ms[hd] - m_new)
                p = jnp.exp2(s_cur[hd] - m_new).astype(_BF16)
                accs[hd] = alpha * accs[hd] + _dot(vt_ref[0, hd, j], p)
                ms[hd] = m_new
        out_t = jnp.concatenate([accs[hd][:C_V] * (1.0 / accs[hd][C_V:C_V + 1]) for hd in heads], axis=0)
        o_ref[0, rows, :] = out_t.T.astype(o_ref.dtype)


def _latent(qc, kc, vt):
    B, H, S, _ = qc.shape
    nkb = S // C_KBLOCK
    return pl.pallas_call(
        functools.partial(_latent_kernel, seq=S),
        out_shape=jax.ShapeDtypeStruct((B, S, C_WIDTH), _BF16),
        grid=(B, H // C_PAIR, S // C_QSTEP),
        in_specs=[pl.BlockSpec((1, C_PAIR, C_QSTEP, C_PAD), lambda b, g, i: (b, g, i, 0)),
                  pl.BlockSpec((1, C_PAIR, S, C_PAD), lambda b, g, i: (b, g, 0, 0)),
                  pl.BlockSpec((1, C_PAIR, nkb, C_VROWS, C_KBLOCK), lambda b, g, i: (b, g, 0, 0, 0))],
        out_specs=pl.BlockSpec((1, C_QSTEP, C_PAIR * C_V), lambda b, g, i: (b, i, g)),
        compiler_params=pltpu.CompilerParams(dimension_semantics=("arbitrary", "arbitrary", "arbitrary"),
                                             vmem_limit_bytes=VMEM_LIMIT),
        name="latent_attention",
    )(qc, kc, vt)


def _out_kernel(x_ref, g_ref, gc_ref, o1_ref, o2_ref, o3_ref, l1_ref, l2_ref, l3_ref, yb_ref, yc_ref,
                w_o_ref, post_g_ref, out_ref, y_scr, pat_scr):
    outs, lses = [], []
    for p, ((_, d), o_ref, l_ref) in enumerate(zip(A_PATTERNS, (o1_ref, o2_ref, o3_ref), (l1_ref, l2_ref, l3_ref))):
        if d == 1:
            outs.append(o_ref[0].astype(_F32))
            lses.append(l_ref[0])
            continue
        rows = ROW_TILE // d
        nchunk = A_WIDTH // LANES
        for r in range(d):
            for c in range(nchunk):
                lanes = slice(r * A_WIDTH + c * LANES, r * A_WIDTH + (c + 1) * LANES)
                pat_scr[2 * p, c, pl.ds(r, rows, stride=d), :] = o_ref[0, :, lanes].astype(_F32)
                pat_scr[2 * p + 1, c, pl.ds(r, rows, stride=d), :] = l_ref[0, :, lanes]
        outs.append(jnp.concatenate([pat_scr[2 * p, c] for c in range(nchunk)], axis=-1))
        lses.append(jnp.concatenate([pat_scr[2 * p + 1, c] for c in range(nchunk)], axis=-1))
    l1, l2, l3 = lses
    big = jnp.maximum(jnp.maximum(l1, l2), l3)
    e1, e2, e3 = jnp.exp2(l1 - big), jnp.exp2(l2 - big), jnp.exp2(l3 - big)
    num = e1 * outs[0] + e2 * outs[1] + e3 * outs[2]
    ya = num / (e1 + e2 + e3)
    y_scr[:, :A_WIDTH] = (ya * g_ref[0].astype(_F32)).astype(_BF16)
    y_scr[:, A_WIDTH:A_WIDTH + B_WIDTH] = yb_ref[0]
    y_scr[:, A_WIDTH + B_WIDTH:] = yc_ref[0] * gc_ref[0, :, D_MIX // 2 - C_WIDTH:]
    y = _dot(y_scr[...], w_o_ref[0])
    out_ref[0] = x_ref[0] + _rms(y, post_g_ref[0])


def _out_proj(layer, x, gates, o_pats, lse_pats, yb, yc, w_o, post_g):
    B, S, _ = x.shape
    row = lambda n: pl.BlockSpec((1, ROW_TILE, n), lambda b, i: (b, i, 0))
    full = lambda a: _layer_block(a, layer)
    return pl.pallas_call(
        _out_kernel,
        out_shape=jax.ShapeDtypeStruct(x.shape, x.dtype),
        grid=(B, S // ROW_TILE),
        in_specs=[row(D_MODEL), row(A_WIDTH), pl.BlockSpec((1, ROW_TILE, D_MIX // 2), lambda b, i: (b, i, 1))]
                 + [_residue_view_rows(d) for _, d in A_PATTERNS] * 2
                 + [row(B_WIDTH), row(C_WIDTH), full(w_o), full(post_g)],
        out_specs=row(D_MODEL),
        scratch_shapes=[pltpu.VMEM((ROW_TILE, D_MIX), _BF16),
                        pltpu.VMEM((2 * len(A_PATTERNS), A_WIDTH // LANES, ROW_TILE, LANES), _F32)],
        compiler_params=pltpu.CompilerParams(dimension_semantics=("arbitrary", "arbitrary"),
                                             vmem_limit_bytes=VMEM_LIMIT),
        name="out_proj",
    )(x, gates, gates, *o_pats, *lse_pats, yb, yc, w_o, post_g)


def _pair_heads(t, axis):
    parts = jnp.split(t, B_HEADS, axis=axis)
    return jnp.concatenate([parts[h] for h in B_HEAD_ORDER], axis=axis)


def _rot_cols(w):
    half = C_ROPE // 2
    return jnp.concatenate([-w[..., half:], w[..., :half]], axis=-1)


def _stacked_weights(w_in, w_uq, w_ukv, w_o):
    depth = w_in.shape[0]
    o = 0
    cols = {}
    for name, n in (("qa", A_WIDTH), ("ka", A_WIDTH), ("va", A_WIDTH), ("ga", A_WIDTH),
                    ("qb", B_WIDTH), ("kb", B_KV_WIDTH), ("vb", B_KV_WIDTH), ("gb", B_WIDTH),
                    ("cq", Q_LORA), ("ckv", KV_LORA), ("kr", C_ROPE), ("gc", C_WIDTH)):
        cols[name] = w_in[..., o:o + n]
        o += n
    w_a = jnp.concatenate([cols["qa"], cols["ka"], cols["va"]], axis=-1)
    w_g = jnp.concatenate([cols["ga"], _pair_heads(cols["gb"], -1), cols["gc"]], axis=-1)
    w_b = jnp.concatenate([_pair_heads(cols["qb"], -1), cols["kb"], cols["vb"]], axis=-1)
    w_c = jnp.concatenate([cols["cq"], cols["ckv"], jnp.zeros((depth, D_MODEL, C_NOPE), w_in.dtype),
                           cols["kr"], _rot_cols(cols["kr"])], axis=-1)
    uq = w_uq.reshape(depth, Q_LORA, C_HEADS, C_NOPE + C_ROPE)
    uq = jnp.concatenate([uq, _rot_cols(uq[..., C_NOPE:])], axis=-1).reshape(depth, Q_LORA, C_HEADS * C_PAD)
    ukv = w_ukv.reshape(depth, KV_LORA, C_HEADS, C_NOPE + C_V)
    uk = jnp.concatenate([ukv[..., :C_NOPE], jnp.zeros((depth, KV_LORA, C_HEADS, C_PAD - C_NOPE), w_ukv.dtype)],
                         axis=-1).reshape(depth, KV_LORA, C_HEADS * C_PAD)
    uvt = jnp.transpose(ukv[..., C_NOPE:], (0, 2, 3, 1))
    uvt = jnp.concatenate([uvt, jnp.zeros((depth, C_HEADS, C_VROWS - C_V, KV_LORA), w_ukv.dtype)], axis=2)
    uvt = uvt.reshape(depth, C_HEADS * C_VROWS, KV_LORA)

    w_o_p = jnp.concatenate([w_o[:, :A_WIDTH], _pair_heads(w_o[:, A_WIDTH:A_WIDTH + B_WIDTH], 1),
                             w_o[:, A_WIDTH + B_WIDTH:]], axis=1)
    bf = lambda t: t.astype(_BF16)
    return (bf(w_a), bf(w_g), bf(w_b), bf(w_c), bf(uq), bf(uk), bf(uvt)), bf(w_o_p)


def kernel(x, positions, pre_norm, w_in, q_a_norm, kv_a_norm, w_uq, w_ukv, sink, w_o, post_norm):
    depth = w_in.shape[0]
    slopes_a, slopes_b = _alibi_slopes()
    cos_t, sin_t = _rope_tables(positions)
    in_w, w_o_p = _stacked_weights(w_in, w_uq, w_ukv, w_o)
    gains = [g[:, None, :] for g in (pre_norm, q_a_norm, kv_a_norm, post_norm)]
    for i in range(depth):
        qkv_a, (gates, qb, kb, vb, qc, kc, vt) = _in_proj(i, x, cos_t, sin_t, *gains[:3], *in_w)
        pats = [_dilated_pattern(*qkv, d, slopes_a) for qkv, (_, d) in zip(qkv_a, A_PATTERNS)]
        yb = _windowed(qb, kb, vb, gates, sink[i], slopes_b)
        yc = _latent(qc, kc, vt)
        x = _out_proj(i, x, gates, [p[0] for p in pats], [p[1] for p in pats], yb, yc, w_o_p, gains[3])
    return x
```

```python
import functools
import math

import numpy as np
import jax
import jax.numpy as jnp
from jax import lax
from jax.experimental import pallas as pl
from jax.experimental.pallas import tpu as pltpu

D_MODEL = 1024
HEAD_DIM = 64
A_HEADS = 4
A_WIDTH = A_HEADS * HEAD_DIM
A_PATTERNS = ((128, 1), (512, 4), (2048, 16))
A_HALF = 64
B_HEADS = 6
B_KV_HEADS = 2
B_GROUP = B_HEADS // B_KV_HEADS
B_WIDTH = B_HEADS * HEAD_DIM
B_KV_WIDTH = B_KV_HEADS * HEAD_DIM
B_WINDOW = 128
C_HEADS = 6
C_NOPE = 64
C_ROPE = 32
C_V = 64
C_WIDTH = C_HEADS * C_V
Q_LORA = 256
KV_LORA = 128
ROPE_THETA = 10000.0
D_MIX = A_WIDTH + B_WIDTH + C_WIDTH
N_ALIBI = A_HEADS + B_HEADS
RMS_EPS = 1e-6
NEG_INF = -1e30
LOG2E = math.log2(math.e)

LANES = 128
C_PAD = LANES
C_VROWS = 80
C_PAIR = 2

ROW_TILE = 1024
A_QBLOCK = 128
A_KWIN = A_QBLOCK + 2 * A_HALF
A_RESIDUES_PER_STEP = 8
A_LOOP_UNROLL = 8
B_QBLOCK = 128
B_KWIN = B_QBLOCK + 2 * B_WINDOW
B_QBLOCKS_PER_STEP = 16
B_QSTEP = B_QBLOCKS_PER_STEP * B_QBLOCK
C_QBLOCK = 512
C_KBLOCK = 256
C_QBLOCKS_PER_STEP = 4
C_SCORE_LOOKAHEAD = 1
C_QSTEP = C_QBLOCKS_PER_STEP * C_QBLOCK
VMEM_LIMIT = 48 * 1024 * 1024

B_HEAD_ORDER = (0, 3, 1, 4, 2, 5)

_F32 = jnp.float32
_BF16 = jnp.bfloat16


def _alibi_slopes():
    s = 2.0 ** (-8.0 * np.arange(1, N_ALIBI + 1, dtype=np.float64) / N_ALIBI)
    return [float(v) for v in s[B_HEADS:]], [float(v) for v in s[:B_HEADS]]


def _rms(x, g):
    return x * lax.rsqrt(jnp.mean(x * x, axis=-1, keepdims=True) + RMS_EPS) * g


def _residue_view_shape(batch, seq, d, dtype):
    return jax.ShapeDtypeStruct((batch, seq // d, d * A_WIDTH), dtype)


def _layer_block(stacked, layer):
    return pl.BlockSpec((1,) + stacked.shape[1:], lambda b, i: (layer,) + (0,) * (stacked.ndim - 1))


def _residue_view_rows(d):
    return pl.BlockSpec((1, ROW_TILE // d, d * A_WIDTH), lambda b, i: (b, i, 0))


def _dot(a, b):
    return jnp.dot(a, b, preferred_element_type=_F32)


def _dot_nt(a, b):
    return lax.dot_general(a, b, (((1,), (1,)), ((), ())), preferred_element_type=_F32)


def _rope_table_kernel(pos_ref, freq_ref, cos_ref, sin_ref):
    ang = freq_ref[...] * pos_ref[0].astype(_F32)
    cos, sin = jnp.cos(ang), jnp.sin(ang)
    tokens = ang.shape[1]
    ones = jnp.ones((C_NOPE, tokens), _F32)
    zeros = jnp.zeros((C_NOPE, tokens), _F32)
    tail = zeros[:C_PAD - C_NOPE - C_ROPE]
    cos_ref[0] = jnp.concatenate([ones, cos, cos, tail], axis=0).T
    sin_ref[0] = jnp.concatenate([zeros, sin, sin, tail], axis=0).T


def _rope_tables(positions):
    B, S = positions.shape
    half = C_ROPE // 2
    freq = ROPE_THETA ** (-2.0 * jnp.arange(half, dtype=_F32) / C_ROPE)
    out = jax.ShapeDtypeStruct((B, S, C_PAD), _F32)
    return pl.pallas_call(
        _rope_table_kernel,
        out_shape=(out, out),
        grid=(B, S // ROW_TILE),
        in_specs=[pl.BlockSpec((1, 1, ROW_TILE), lambda b, i: (b, 0, i)),
                  pl.BlockSpec((half, 1), lambda b, i: (0, 0))],
        out_specs=(pl.BlockSpec((1, ROW_TILE, C_PAD), lambda b, i: (b, i, 0)),
                   pl.BlockSpec((1, ROW_TILE, C_PAD), lambda b, i: (b, i, 0))),
        compiler_params=pltpu.CompilerParams(dimension_semantics=("arbitrary", "arbitrary")),
        name="rope_tables",
    )(positions.reshape(B, 1, S), freq.reshape(half, 1))


def _in_proj_kernel(x_ref, cos_ref, sin_ref, pre_g_ref, qn_g_ref, kvn_g_ref,
                    w_a_ref, w_g_ref, w_b_ref, w_c_ref, w_uq_ref, w_uk_ref, w_uvt_ref,
                    *rest):
    a_refs, (g_ref, qb_ref, kb_ref, vb_ref, qc_ref, kc_ref, vt_ref, a_scr) = rest[:3 * len(A_PATTERNS)], rest[-8:]
    h = _rms(x_ref[0], pre_g_ref[0]).astype(_BF16)
    score_scale = HEAD_DIM ** -0.5 * LOG2E

    pa = _dot(h, w_a_ref[0])
    nchunk = 3 * A_WIDTH // LANES
    for c in range(nchunk):
        chunk = pa[:, c * LANES:(c + 1) * LANES]
        a_scr[c] = chunk * score_scale if c < A_WIDTH // LANES else chunk
    for p, (_, d) in enumerate(A_PATTERNS):
        rows = ROW_TILE // d
        for r in range(d):
            for c in range(nchunk):
                res = a_scr[c, pl.ds(r, rows, stride=d), :] if d > 1 else a_scr[c]
                lane0 = r * A_WIDTH + (c * LANES) % A_WIDTH
                a_refs[3 * p + c * LANES // A_WIDTH][0, :, lane0:lane0 + LANES] = res.astype(_BF16)

    for c in range(0, D_MIX, 256):
        g = _dot(h, w_g_ref[0, :, c:c + 256])
        g_ref[0, :, c:c + 256] = (g / (1.0 + jnp.exp(-g))).astype(_BF16)

    pb = _dot(h, w_b_ref[0])
    qb_ref[0] = (pb[:, :B_WIDTH] * score_scale).astype(_BF16)
    kb_ref[0] = pb[:, B_WIDTH:B_WIDTH + B_KV_WIDTH].astype(_BF16)
    vb_ref[0] = pb[:, B_WIDTH + B_KV_WIDTH:].astype(_BF16)

    pc = _dot(h, w_c_ref[0])
    cos = cos_ref[0]
    sin = sin_ref[0]

    def rotary(t):
        return t * cos + pltpu.roll(t, C_PAD - C_ROPE, axis=1) * sin

    cq = _rms(pc[:, :Q_LORA], qn_g_ref[0]).astype(_BF16)
    ckv = _rms(pc[:, Q_LORA:Q_LORA + KV_LORA], kvn_g_ref[0]).astype(_BF16)
    k_rope = rotary(pc[:, Q_LORA + KV_LORA:])
    c_scale = (C_NOPE + C_ROPE) ** -0.5 * LOG2E
    for pair in range(C_HEADS // 2):
        cols = slice(2 * pair * C_PAD, 2 * (pair + 1) * C_PAD)
        q_pair = _dot(cq, w_uq_ref[0, :, cols])
        k_pair = _dot(ckv, w_uk_ref[0, :, cols])
        for side in range(2):
            lanes = slice(side * C_PAD, (side + 1) * C_PAD)
            qc_ref[0, 2 * pair + side] = (rotary(q_pair[:, lanes]) * c_scale).astype(_BF16)
            kc_ref[0, 2 * pair + side] = (k_pair[:, lanes] + k_rope).astype(_BF16)
    vt = _dot_nt(w_uvt_ref[0], ckv)
    row = lax.broadcasted_iota(jnp.int32, vt.shape, 0)
    ones_row = functools.reduce(jnp.logical_or, [row == hd * C_VROWS + C_V for hd in range(C_HEADS)])
    vt = jnp.where(ones_row, 1.0, vt).astype(_BF16)
    for hd in range(C_HEADS):
        for c in range(ROW_TILE // C_KBLOCK):
            vt_ref[0, hd, c] = vt[hd * C_VROWS:(hd + 1) * C_VROWS, c * C_KBLOCK:(c + 1) * C_KBLOCK]


def _in_proj(layer, x, cos_t, sin_t, pre_g, qn_g, kvn_g, w_a, w_g, w_b, w_c, w_uq, w_uk, w_uvt):
    B, S, _ = x.shape
    nblk = S // ROW_TILE
    row = lambda n: pl.BlockSpec((1, ROW_TILE, n), lambda b, i: (b, i, 0))
    full = lambda a: _layer_block(a, layer)
    bf = lambda n: jax.ShapeDtypeStruct((B, S, n), _BF16)
    head4 = pl.BlockSpec((1, C_HEADS, ROW_TILE, C_PAD), lambda b, i: (b, 0, i, 0))
    weights = (pre_g, qn_g, kvn_g, w_a, w_g, w_b, w_c, w_uq, w_uk, w_uvt)
    a_shapes = tuple(_residue_view_shape(B, S, d, _BF16) for _, d in A_PATTERNS for _ in range(3))
    a_specs = tuple(_residue_view_rows(d) for _, d in A_PATTERNS for _ in range(3))
    outs = pl.pallas_call(
        _in_proj_kernel,
        out_shape=a_shapes + (bf(D_MIX), bf(B_WIDTH), bf(B_KV_WIDTH), bf(B_KV_WIDTH),
                              jax.ShapeDtypeStruct((B, C_HEADS, S, C_PAD), _BF16),
                              jax.ShapeDtypeStruct((B, C_HEADS, S, C_PAD), _BF16),
                              jax.ShapeDtypeStruct((B, C_HEADS, S // C_KBLOCK, C_VROWS, C_KBLOCK), _BF16)),
        grid=(B, nblk),
        in_specs=[row(D_MODEL), row(C_PAD), row(C_PAD)] + [full(w) for w in weights],
        out_specs=a_specs + (row(D_MIX), row(B_WIDTH), row(B_KV_WIDTH), row(B_KV_WIDTH), head4, head4,
                             pl.BlockSpec((1, C_HEADS, ROW_TILE // C_KBLOCK, C_VROWS, C_KBLOCK),
                                          lambda b, i: (b, 0, i, 0, 0))),
        scratch_shapes=[pltpu.VMEM((3 * A_WIDTH // LANES, ROW_TILE, LANES), _F32)],
        compiler_params=pltpu.CompilerParams(dimension_semantics=("arbitrary", "arbitrary"),
                                             vmem_limit_bytes=VMEM_LIMIT),
        name="in_proj",
    )(x, cos_t, sin_t, *weights)
    n_a = 3 * len(A_PATTERNS)
    return [outs[3 * p:3 * p + 3] for p in range(len(A_PATTERNS))], outs[n_a:]


def _dilated_kernel(q_ref, k_ref, v_ref, o_ref, lse_ref, bias_ref, *, length, dilation, group, slopes):
    nblk = length // A_QBLOCK
    lane_head = lax.shift_right_logical(lax.broadcasted_iota(jnp.int32, (A_QBLOCK, A_WIDTH), 1),
                                        int(math.log2(HEAD_DIM)))

    @pl.when((pl.program_id(0) == 0) & (pl.program_id(1) == 0))
    def _():
        rel0 = (lax.broadcasted_iota(jnp.int32, (A_QBLOCK, A_KWIN), 1)
                - lax.broadcasted_iota(jnp.int32, (A_QBLOCK, A_KWIN), 0))
        for case, shift in enumerate((0, -A_HALF, -2 * A_HALF)):
            dist = jnp.abs(rel0 + shift)
            for hd in range(A_HEADS):
                bias_ref[case, hd] = jnp.where(dist <= A_HALF,
                                               dist.astype(_F32) * (-slopes[hd] * dilation * LOG2E), NEG_INF)

    low = lax.broadcasted_iota(jnp.int32, (A_QBLOCK, LANES), 1) < HEAD_DIM

    def load(blk, lanes):
        t0 = pl.multiple_of(blk * A_QBLOCK, A_QBLOCK)
        start = pl.multiple_of(jnp.clip(t0 - A_HALF, 0, length - A_KWIN), A_HALF)
        case = jnp.where(blk == 0, 0, jnp.where(blk == nblk - 1, 2, 1))
        q = q_ref[0, pl.ds(t0, A_QBLOCK), lanes]
        q_heads = jnp.concatenate([jnp.where(lane_head == hd, q, jnp.zeros_like(q)) for hd in range(A_HEADS)],
                                  axis=0)
        return t0, case, q_heads, k_ref[0, pl.ds(start, A_KWIN), lanes], v_ref[0, pl.ds(start, A_KWIN), lanes]

    def softmax(s_all, case):
        ps, ms, ls = [], [], []
        for hd in range(A_HEADS):
            s = s_all[hd * A_QBLOCK:(hd + 1) * A_QBLOCK] + bias_ref[case, hd]
            m = jnp.max(s, axis=-1, keepdims=True)
            p = jnp.exp2(s - m)
            ls.append(jnp.sum(p, axis=-1, keepdims=True))
            ms.append(m)
            ps.append(p.astype(_BF16))
        return ps, ms, ls

    def values(t0, lanes, ps, ms, ls, vw):
        outs, lses = [], []
        for pair in range(A_HEADS // 2):
            h0, h1 = 2 * pair, 2 * pair + 1
            acc = _dot(jnp.concatenate([ps[h0], ps[h1]], axis=0), vw[:, pair * LANES:(pair + 1) * LANES])
            l_pair = jnp.where(low, ls[h0], ls[h1])
            outs.append(jnp.where(low, acc[:A_QBLOCK], acc[A_QBLOCK:]) * (1.0 / l_pair))
            lses.append(jnp.where(low, ms[h0], ms[h1]) + jnp.log2(l_pair))
        o_ref[0, pl.ds(t0, A_QBLOCK), lanes] = jnp.concatenate(outs, axis=-1).astype(o_ref.dtype)
        lse_ref[0, pl.ds(t0, A_QBLOCK), lanes] = jnp.concatenate(lses, axis=-1)

    def two_blocks(i, lanes):
        blocks = [load(2 * i + j, lanes) for j in range(2)]
        scores = [_dot_nt(q_heads, kw) for _, _, q_heads, kw, _ in blocks]
        for (t0, case, _, _, vw), s_all in zip(blocks, scores):
            values(t0, lanes, *softmax(s_all, case), vw)

    def residue_class(lanes):
        def body(i, carry):
            two_blocks(i, lanes)
            return carry
        lax.fori_loop(0, nblk // 2, body, 0, unroll=min(nblk // 2, A_LOOP_UNROLL))

    for g in range(group):
        residue_class(slice(g * A_WIDTH, (g + 1) * A_WIDTH))


def _dilated_pattern(q, k, v, dilation, slopes):
    B, length, _ = q.shape
    group = min(dilation, A_RESIDUES_PER_STEP)
    spec = pl.BlockSpec((1, length, group * A_WIDTH), lambda b, r: (b, 0, r))
    return pl.pallas_call(
        functools.partial(_dilated_kernel, length=length, dilation=dilation, group=group, slopes=slopes),
        out_shape=(jax.ShapeDtypeStruct(q.shape, _BF16), jax.ShapeDtypeStruct(q.shape, _F32)),
        grid=(B, dilation // group),
        in_specs=[spec, spec, spec],
        out_specs=(spec, spec),
        scratch_shapes=[pltpu.VMEM((3, A_HEADS, A_QBLOCK, A_KWIN), _F32)],
        compiler_params=pltpu.CompilerParams(dimension_semantics=("arbitrary", "arbitrary"),
                                             vmem_limit_bytes=VMEM_LIMIT),
        name=f"dilated_d{dilation}",
    )(q, k, v)


def _windowed_kernel(sink_ref, q_ref, k_ref, v_ref, *rest, seq, slopes):
    gate_refs, (o_ref, bias_ref) = rest[:B_HEADS // 2], rest[B_HEADS // 2:]
    nblk = seq // B_QBLOCK

    @pl.when((pl.program_id(0) == 0) & (pl.program_id(1) == 0))
    def _():
        rel0 = (lax.broadcasted_iota(jnp.int32, (B_QBLOCK, B_KWIN), 1)
                - lax.broadcasted_iota(jnp.int32, (B_QBLOCK, B_KWIN), 0))
        for case, shift in enumerate((0, -B_WINDOW, -2 * B_WINDOW)):
            dist = jnp.abs(rel0 + shift)
            for idx, head in enumerate(B_HEAD_ORDER):
                bias_ref[case, idx] = jnp.where(dist <= B_WINDOW, dist.astype(_F32) * (-slopes[head] * LOG2E),
                                                NEG_INF)

    low = lax.broadcasted_iota(jnp.int32, (B_QBLOCK, LANES), 1) < HEAD_DIM
    npair = B_HEADS // 2
    blocks = []
    for j in range(B_QBLOCKS_PER_STEP):
        blk = pl.program_id(1) * B_QBLOCKS_PER_STEP + j
        t0 = pl.multiple_of(blk * B_QBLOCK, B_QBLOCK)
        start = pl.multiple_of(jnp.clip(t0 - B_WINDOW, 0, seq - B_KWIN), B_WINDOW)
        case = jnp.where(blk == 0, 0, jnp.where(blk == nblk - 1, 2, 1))
        rows = slice(j * B_QBLOCK, (j + 1) * B_QBLOCK)
        blocks.append((rows, case, k_ref[0, pl.ds(start, B_KWIN), :], v_ref[0, pl.ds(start, B_KWIN), :]))

    def scores(item):
        (rows, _, kw, _), pair = item
        q = q_ref[0, rows, pair * LANES:(pair + 1) * LANES]
        zero = jnp.zeros_like(q)
        return _dot_nt(jnp.concatenate([jnp.where(low, q, zero), jnp.where(low, zero, q)], axis=0), kw)

    items = [(block, pair) for block in blocks for pair in range(npair)]
    s_next = scores(items[0])
    for n, ((rows, case, _, vw), pair) in enumerate(items):
        s_pair, s_next = s_next, (scores(items[n + 1]) if n + 1 < len(items) else None)
        ps, ms, ls, sinks = [], [], [], []
        for side in range(2):
            idx = 2 * pair + side
            s = s_pair[side * B_QBLOCK:(side + 1) * B_QBLOCK] + bias_ref[case, idx]
            m = jnp.max(s, axis=-1, keepdims=True)
            p = jnp.exp2(s - m)
            ls.append(jnp.sum(p, axis=-1, keepdims=True))
            ms.append(m)
            sinks.append(sink_ref[B_HEAD_ORDER[idx]] * LOG2E)
            ps.append(p.astype(_BF16))
        acc = _dot(jnp.concatenate(ps, axis=0), vw)
        factor = 1.0 / (jnp.where(low, ls[0], ls[1])
                        + jnp.exp2(jnp.where(low, sinks[0], sinks[1]) - jnp.where(low, ms[0], ms[1])))
        out = jnp.where(low, acc[:B_QBLOCK], acc[B_QBLOCK:]) * factor * gate_refs[pair][0, rows, :].astype(_F32)
        o_ref[0, rows, pair * LANES:(pair + 1) * LANES] = out.astype(o_ref.dtype)


def _windowed(qb, kb, vb, gates, sink, slopes):
    B, S, _ = qb.shape
    kv_spec = pl.BlockSpec((1, S, B_KV_WIDTH), lambda b, i, sink: (b, 0, 0))
    q_spec = pl.BlockSpec((1, B_QSTEP, B_WIDTH), lambda b, i, sink: (b, i, 0))
    gate_specs = [pl.BlockSpec((1, B_QSTEP, LANES), lambda b, i, sink, blk=A_WIDTH // LANES + pair: (b, i, blk))
                  for pair in range(B_HEADS // 2)]
    return pl.pallas_call(
        functools.partial(_windowed_kernel, seq=S, slopes=slopes),
        out_shape=jax.ShapeDtypeStruct((B, S, B_WIDTH), _BF16),
        grid_spec=pltpu.PrefetchScalarGridSpec(
            num_scalar_prefetch=1, grid=(B, S // B_QSTEP),
            in_specs=[q_spec, kv_spec, kv_spec] + gate_specs, out_specs=q_spec,
            scratch_shapes=[pltpu.VMEM((3, B_HEADS, B_QBLOCK, B_KWIN), _F32)]),
        compiler_params=pltpu.CompilerParams(dimension_semantics=("arbitrary", "arbitrary"),
                                             vmem_limit_bytes=VMEM_LIMIT),
        name="windowed_gqa",
    )(sink, qb, kb, vb, *[gates] * (B_HEADS // 2))


def _latent_kernel(q_ref, k_ref, vt_ref, o_ref, *, seq):
    nkb = seq // C_KBLOCK
    heads = range(C_PAIR)

    for qb in range(C_QBLOCKS_PER_STEP):
        rows = slice(qb * C_QBLOCK, (qb + 1) * C_QBLOCK)
        qs = [q_ref[0, hd, rows, :] for hd in heads]

        def scores(hd, j):
            return _dot_nt(k_ref[0, hd, j * C_KBLOCK:(j + 1) * C_KBLOCK, :], qs[hd])

        ms = [jnp.full((1, C_QBLOCK), NEG_INF, _F32) for _ in heads]
        accs = [jnp.zeros((C_VROWS, C_QBLOCK), _F32) for _ in heads]
        pending = [[scores(hd, j) for hd in heads] for j in range(C_SCORE_LOOKAHEAD)]
        for j in range(nkb):
            if j + C_SCORE_LOOKAHEAD < nkb:
                pending.append([scores(hd, j + C_SCORE_LOOKAHEAD) for hd in heads])
            s_cur = pending.pop(0)
            for hd in heads:
                m_new = jnp.maximum(ms[hd], jnp.max(s_cur[hd], axis=0, keepdims=True))
                alpha = jnp.exp2(ms[hd] - m_new)
                p = jnp.exp2(s_cur[hd] - m_new).astype(_BF16)
                accs[hd] = alpha * accs[hd] + _dot(vt_ref[0, hd, j], p)
                ms[hd] = m_new
        out_t = jnp.concatenate([accs[hd][:C_V] * (1.0 / accs[hd][C_V:C_V + 1]) for hd in heads], axis=0)
        o_ref[0, rows, :] = out_t.T.astype(o_ref.dtype)


def _latent(qc, kc, vt):
    B, H, S, _ = qc.shape
    nkb = S // C_KBLOCK
    return pl.pallas_call(
        functools.partial(_latent_kernel, seq=S),
        out_shape=jax.ShapeDtypeStruct((B, S, C_WIDTH), _BF16),
        grid=(B, H // C_PAIR, S // C_QSTEP),
        in_specs=[pl.BlockSpec((1, C_PAIR, C_QSTEP, C_PAD), lambda b, g, i: (b, g, i, 0)),
                  pl.BlockSpec((1, C_PAIR, S, C_PAD), lambda b, g, i: (b, g, 0, 0)),
                  pl.BlockSpec((1, C_PAIR, nkb, C_VROWS, C_KBLOCK), lambda b, g, i: (b, g, 0, 0, 0))],
        out_specs=pl.BlockSpec((1, C_QSTEP, C_PAIR * C_V), lambda b, g, i: (b, i, g)),
        compiler_params=pltpu.CompilerParams(dimension_semantics=("arbitrary", "arbitrary", "arbitrary"),
                                             vmem_limit_bytes=VMEM_LIMIT),
        name="latent_attention",
    )(qc, kc, vt)


def _out_kernel(x_ref, g_ref, gc_ref, o1_ref, o2_ref, o3_ref, l1_ref, l2_ref, l3_ref, yb_ref, yc_ref,
                w_o_ref, post_g_ref, out_ref, y_scr, pat_scr):
    outs, lses = [], []
    for p, ((_, d), o_ref, l_ref) in enumerate(zip(A_PATTERNS, (o1_ref, o2_ref, o3_ref), (l1_ref, l2_ref, l3_ref))):
        if d == 1:
            outs.append(o_ref[0].astype(_F32))
            lses.append(l_ref[0])
            continue
        rows = ROW_TILE // d
        nchunk = A_WIDTH // LANES
        for r in range(d):
            for c in range(nchunk):
                lanes = slice(r * A_WIDTH + c * LANES, r * A_WIDTH + (c + 1) * LANES)
                pat_scr[2 * p, c, pl.ds(r, rows, stride=d), :] = o_ref[0, :, lanes].astype(_F32)
                pat_scr[2 * p + 1, c, pl.ds(r, rows, stride=d), :] = l_ref[0, :, lanes]
        outs.append(jnp.concatenate([pat_scr[2 * p, c] for c in range(nchunk)], axis=-1))
        lses.append(jnp.concatenate([pat_scr[2 * p + 1, c] for c in range(nchunk)], axis=-1))
    l1, l2, l3 = lses
    big = jnp.maximum(jnp.maximum(l1, l2), l3)
    e1, e2, e3 = jnp.exp2(l1 - big), jnp.exp2(l2 - big), jnp.exp2(l3 - big)
    num = e1 * outs[0] + e2 * outs[1] + e3 * outs[2]
    ya = num / (e1 + e2 + e3)
    y_scr[:, :A_WIDTH] = (ya * g_ref[0].astype(_F32)).astype(_BF16)
    y_scr[:, A_WIDTH:A_WIDTH + B_WIDTH] = yb_ref[0]
    y_scr[:, A_WIDTH + B_WIDTH:] = yc_ref[0] * gc_ref[0, :, D_MIX // 2 - C_WIDTH:]
    y = _dot(y_scr[...], w_o_ref[0])
    out_ref[0] = x_ref[0] + _rms(y, post_g_ref[0])


def _out_proj(layer, x, gates, o_pats, lse_pats, yb, yc, w_o, post_g):
    B, S, _ = x.shape
    row = lambda n: pl.BlockSpec((1, ROW_TILE, n), lambda b, i: (b, i, 0))
    full = lambda a: _layer_block(a, layer)
    return pl.pallas_call(
        _out_kernel,
        out_shape=jax.ShapeDtypeStruct(x.shape, x.dtype),
        grid=(B, S // ROW_TILE),
        in_specs=[row(D_MODEL), row(A_WIDTH), pl.BlockSpec((1, ROW_TILE, D_MIX // 2), lambda b, i: (b, i, 1))]
                 + [_residue_view_rows(d) for _, d in A_PATTERNS] * 2
                 + [row(B_WIDTH), row(C_WIDTH), full(w_o), full(post_g)],
        out_specs=row(D_MODEL),
        scratch_shapes=[pltpu.VMEM((ROW_TILE, D_MIX), _BF16),
                        pltpu.VMEM((2 * len(A_PATTERNS), A_WIDTH // LANES, ROW_TILE, LANES), _F32)],
        compiler_params=pltpu.CompilerParams(dimension_semantics=("arbitrary", "arbitrary"),
                                             vmem_limit_bytes=VMEM_LIMIT),
        name="out_proj",
    )(x, gates, gates, *o_pats, *lse_pats, yb, yc, w_o, post_g)


def _pair_heads(t, axis):
    parts = jnp.split(t, B_HEADS, axis=axis)
    return jnp.concatenate([parts[h] for h in B_HEAD_ORDER], axis=axis)


def _rot_cols(w):
    half = C_ROPE // 2
    return jnp.concatenate([-w[..., half:], w[..., :half]], axis=-1)


def _stacked_weights(w_in, w_uq, w_ukv, w_o):
    depth = w_in.shape[0]
    o = 0
    cols = {}
    for name, n in (("qa", A_WIDTH), ("ka", A_WIDTH), ("va", A_WIDTH), ("ga", A_WIDTH),
                    ("qb", B_WIDTH), ("kb", B_KV_WIDTH), ("vb", B_KV_WIDTH), ("gb", B_WIDTH),
                    ("cq", Q_LORA), ("ckv", KV_LORA), ("kr", C_ROPE), ("gc", C_WIDTH)):
        cols[name] = w_in[..., o:o + n]
        o += n
    w_a = jnp.concatenate([cols["qa"], cols["ka"], cols["va"]], axis=-1)
    w_g = jnp.concatenate([cols["ga"], _pair_heads(cols["gb"], -1), cols["gc"]], axis=-1)
    w_b = jnp.concatenate([_pair_heads(cols["qb"], -1), cols["kb"], cols["vb"]], axis=-1)
    w_c = jnp.concatenate([cols["cq"], cols["ckv"], jnp.zeros((depth, D_MODEL, C_NOPE), w_in.dtype),
                           cols["kr"], _rot_cols(cols["kr"])], axis=-1)
    uq = w_uq.reshape(depth, Q_LORA, C_HEADS, C_NOPE + C_ROPE)
    uq = jnp.concatenate([uq, _rot_cols(uq[..., C_NOPE:])], axis=-1).reshape(depth, Q_LORA, C_HEADS * C_PAD)
    ukv = w_ukv.reshape(depth, KV_LORA, C_HEADS, C_NOPE + C_V)
    uk = jnp.concatenate([ukv[..., :C_NOPE], jnp.zeros((depth, KV_LORA, C_HEADS, C_PAD - C_NOPE), w_ukv.dtype)],
                         axis=-1).reshape(depth, KV_LORA, C_HEADS * C_PAD)
    uvt = jnp.transpose(ukv[..., C_NOPE:], (0, 2, 3, 1))
    uvt = jnp.concatenate([uvt, jnp.zeros((depth, C_HEADS, C_VROWS - C_V, KV_LORA), w_ukv.dtype)], axis=2)
    uvt = uvt.reshape(depth, C_HEADS * C_VROWS, KV_LORA)

    w_o_p = jnp.concatenate([w_o[:, :A_WIDTH], _pair_heads(w_o[:, A_WIDTH:A_WIDTH + B_WIDTH], 1),
                             w_o[:, A_WIDTH + B_WIDTH:]], axis=1)
    bf = lambda t: t.astype(_BF16)
    return (bf(w_a), bf(w_g), bf(w_b), bf(w_c), bf(uq), bf(uk), bf(uvt)), bf(w_o_p)


def kernel(x, positions, pre_norm, w_in, q_a_norm, kv_a_norm, w_uq, w_ukv, sink, w_o, post_norm):
    depth = w_in.shape[0]
    slopes_a, slopes_b = _alibi_slopes()
    cos_t, sin_t = _rope_tables(positions)
    in_w, w_o_p = _stacked_weights(w_in, w_uq, w_ukv, w_o)
    gains = [g[:, None, :] for g in (pre_norm, q_a_norm, kv_a_norm, post_norm)]
    for i in range(depth):
        qkv_a, (gates, qb, kb, vb, qc, kc, vt) = _in_proj(i, x, cos_t, sin_t, *gains[:3], *in_w)
        pats = [_dilated_pattern(*qkv, d, slopes_a) for qkv, (_, d) in zip(qkv_a, A_PATTERNS)]
        yb = _windowed(qb, kb, vb, gates, sink[i], slopes_b)
        yc = _latent(qc, kc, vt)
        x = _out_proj(i, x, gates, [p[0] for p in pats], [p[1] for p in pats], yb, yc, w_o_p, gains[3])
    return x
```

```python
import functools
import math

import numpy as np
import jax
import jax.numpy as jnp
from jax import lax
from jax.experimental import pallas as pl
from jax.experimental.pallas import tpu as pltpu

D_MODEL = 1024
HEAD_DIM = 64
A_HEADS = 4
A_WIDTH = A_HEADS * HEAD_DIM
A_PATTERNS = ((128, 1), (512, 4), (2048, 16))
A_HALF = 64
B_HEADS = 6
B_KV_HEADS = 2
B_GROUP = B_HEADS // B_KV_HEADS
B_WIDTH = B_HEADS * HEAD_DIM
B_KV_WIDTH = B_KV_HEADS * HEAD_DIM
B_WINDOW = 128
C_HEADS = 6
C_NOPE = 64
C_ROPE = 32
C_V = 64
C_WIDTH = C_HEADS * C_V
Q_LORA = 256
KV_LORA = 128
ROPE_THETA = 10000.0
D_MIX = A_WIDTH + B_WIDTH + C_WIDTH
N_ALIBI = A_HEADS + B_HEADS
RMS_EPS = 1e-6
NEG_INF = -1e30
LOG2E = math.log2(math.e)

LANES = 128
C_PAD = LANES
C_VROWS = 80
C_PAIR = 2

ROW_TILE = 1024
A_QBLOCK = 128
A_KWIN = A_QBLOCK + 2 * A_HALF
A_RESIDUES_PER_STEP = 8
A_LOOP_UNROLL = 8
B_QBLOCK = 128
B_KWIN = B_QBLOCK + 2 * B_WINDOW
B_QBLOCKS_PER_STEP = 16
B_QSTEP = B_QBLOCKS_PER_STEP * B_QBLOCK
C_QBLOCK = 512
C_KBLOCK = 256
C_QBLOCKS_PER_STEP = 8
C_SCORE_LOOKAHEAD = 1
C_QSTEP = C_QBLOCKS_PER_STEP * C_QBLOCK
VMEM_LIMIT = 48 * 1024 * 1024

B_HEAD_ORDER = (0, 3, 1, 4, 2, 5)

_F32 = jnp.float32
_BF16 = jnp.bfloat16


def _alibi_slopes():
    s = 2.0 ** (-8.0 * np.arange(1, N_ALIBI + 1, dtype=np.float64) / N_ALIBI)
    return [float(v) for v in s[B_HEADS:]], [float(v) for v in s[:B_HEADS]]


def _rms(x, g):
    return x * lax.rsqrt(jnp.mean(x * x, axis=-1, keepdims=True) + RMS_EPS) * g


def _residue_view_shape(batch, seq, d, dtype):
    return jax.ShapeDtypeStruct((batch, seq // d, d * A_WIDTH), dtype)


def _layer_block(stacked, layer):
    return pl.BlockSpec((1,) + stacked.shape[1:], lambda b, i: (layer,) + (0,) * (stacked.ndim - 1))


def _residue_view_rows(d):
    return pl.BlockSpec((1, ROW_TILE // d, d * A_WIDTH), lambda b, i: (b, i, 0))


def _dot(a, b):
    return jnp.dot(a, b, preferred_element_type=_F32)


def _dot_nt(a, b):
    return lax.dot_general(a, b, (((1,), (1,)), ((), ())), preferred_element_type=_F32)


def _rope_table_kernel(pos_ref, freq_ref, cos_ref, sin_ref):
    ang = freq_ref[...] * pos_ref[0].astype(_F32)
    cos, sin = jnp.cos(ang), jnp.sin(ang)
    tokens = ang.shape[1]
    ones = jnp.ones((C_NOPE, tokens), _F32)
    zeros = jnp.zeros((C_NOPE, tokens), _F32)
    tail = zeros[:C_PAD - C_NOPE - C_ROPE]
    cos_ref[0] = jnp.concatenate([ones, cos, cos, tail], axis=0).T
    sin_ref[0] = jnp.concatenate([zeros, sin, sin, tail], axis=0).T


def _rope_tables(positions):
    B, S = positions.shape
    half = C_ROPE // 2
    freq = ROPE_THETA ** (-2.0 * jnp.arange(half, dtype=_F32) / C_ROPE)
    out = jax.ShapeDtypeStruct((B, S, C_PAD), _F32)
    return pl.pallas_call(
        _rope_table_kernel,
        out_shape=(out, out),
        grid=(B, S // ROW_TILE),
        in_specs=[pl.BlockSpec((1, 1, ROW_TILE), lambda b, i: (b, 0, i)),
                  pl.BlockSpec((half, 1), lambda b, i: (0, 0))],
        out_specs=(pl.BlockSpec((1, ROW_TILE, C_PAD), lambda b, i: (b, i, 0)),
                   pl.BlockSpec((1, ROW_TILE, C_PAD), lambda b, i: (b, i, 0))),
        compiler_params=pltpu.CompilerParams(dimension_semantics=("arbitrary", "arbitrary")),
        name="rope_tables",
    )(positions.reshape(B, 1, S), freq.reshape(half, 1))


def _in_proj_kernel(x_ref, cos_ref, sin_ref, pre_g_ref, qn_g_ref, kvn_g_ref,
                    w_a_ref, w_g_ref, w_b_ref, w_c_ref, w_uq_ref, w_uk_ref, w_uvt_ref,
                    *rest):
    a_refs, (g_ref, qb_ref, kb_ref, vb_ref, qc_ref, kc_ref, vt_ref, a_scr) = rest[:3 * len(A_PATTERNS)], rest[-8:]
    h = _rms(x_ref[0], pre_g_ref[0]).astype(_BF16)
    score_scale = HEAD_DIM ** -0.5 * LOG2E

    pa = _dot(h, w_a_ref[0])
    nchunk = 3 * A_WIDTH // LANES
    for c in range(nchunk):
        chunk = pa[:, c * LANES:(c + 1) * LANES]
        a_scr[c] = chunk * score_scale if c < A_WIDTH // LANES else chunk
    for p, (_, d) in enumerate(A_PATTERNS):
        rows = ROW_TILE // d
        for r in range(d):
            for c in range(nchunk):
                res = a_scr[c, pl.ds(r, rows, stride=d), :] if d > 1 else a_scr[c]
                lane0 = r * A_WIDTH + (c * LANES) % A_WIDTH
                a_refs[3 * p + c * LANES // A_WIDTH][0, :, lane0:lane0 + LANES] = res.astype(_BF16)

    for c in range(0, D_MIX, 256):
        g = _dot(h, w_g_ref[0, :, c:c + 256])
        g_ref[0, :, c:c + 256] = (g / (1.0 + jnp.exp(-g))).astype(_BF16)

    pb = _dot(h, w_b_ref[0])
    qb_ref[0] = (pb[:, :B_WIDTH] * score_scale).astype(_BF16)
    kb_ref[0] = pb[:, B_WIDTH:B_WIDTH + B_KV_WIDTH].astype(_BF16)
    vb_ref[0] = pb[:, B_WIDTH + B_KV_WIDTH:].astype(_BF16)

    pc = _dot(h, w_c_ref[0])
    cos = cos_ref[0]
    sin = sin_ref[0]

    def rotary(t):
        return t * cos + pltpu.roll(t, C_PAD - C_ROPE, axis=1) * sin

    cq = _rms(pc[:, :Q_LORA], qn_g_ref[0]).astype(_BF16)
    ckv = _rms(pc[:, Q_LORA:Q_LORA + KV_LORA], kvn_g_ref[0]).astype(_BF16)
    k_rope = rotary(pc[:, Q_LORA + KV_LORA:])
    c_scale = (C_NOPE + C_ROPE) ** -0.5 * LOG2E
    for pair in range(C_HEADS // 2):
        cols = slice(2 * pair * C_PAD, 2 * (pair + 1) * C_PAD)
        q_pair = _dot(cq, w_uq_ref[0, :, cols])
        k_pair = _dot(ckv, w_uk_ref[0, :, cols])
        for side in range(2):
            lanes = slice(side * C_PAD, (side + 1) * C_PAD)
            qc_ref[0, 2 * pair + side] = (rotary(q_pair[:, lanes]) * c_scale).astype(_BF16)
            kc_ref[0, 2 * pair + side] = (k_pair[:, lanes] + k_rope).astype(_BF16)
    vt = _dot_nt(w_uvt_ref[0], ckv)
    row = lax.broadcasted_iota(jnp.int32, vt.shape, 0)
    ones_row = functools.reduce(jnp.logical_or, [row == hd * C_VROWS + C_V for hd in range(C_HEADS)])
    vt = jnp.where(ones_row, 1.0, vt).astype(_BF16)
    for hd in range(C_HEADS):
        for c in range(ROW_TILE // C_KBLOCK):
            vt_ref[0, hd, c] = vt[hd * C_VROWS:(hd + 1) * C_VROWS, c * C_KBLOCK:(c + 1) * C_KBLOCK]


def _in_proj(layer, x, cos_t, sin_t, pre_g, qn_g, kvn_g, w_a, w_g, w_b, w_c, w_uq, w_uk, w_uvt):
    B, S, _ = x.shape
    nblk = S // ROW_TILE
    row = lambda n: pl.BlockSpec((1, ROW_TILE, n), lambda b, i: (b, i, 0))
    full = lambda a: _layer_block(a, layer)
    bf = lambda n: jax.ShapeDtypeStruct((B, S, n), _BF16)
    head4 = pl.BlockSpec((1, C_HEADS, ROW_TILE, C_PAD), lambda b, i: (b, 0, i, 0))
    weights = (pre_g, qn_g, kvn_g, w_a, w_g, w_b, w_c, w_uq, w_uk, w_uvt)
    a_shapes = tuple(_residue_view_shape(B, S, d, _BF16) for _, d in A_PATTERNS for _ in range(3))
    a_specs = tuple(_residue_view_rows(d) for _, d in A_PATTERNS for _ in range(3))
    outs = pl.pallas_call(
        _in_proj_kernel,
        out_shape=a_shapes + (bf(D_MIX), bf(B_WIDTH), bf(B_KV_WIDTH), bf(B_KV_WIDTH),
                              jax.ShapeDtypeStruct((B, C_HEADS, S, C_PAD), _BF16),
                              jax.ShapeDtypeStruct((B, C_HEADS, S, C_PAD), _BF16),
                              jax.ShapeDtypeStruct((B, C_HEADS, S // C_KBLOCK, C_VROWS, C_KBLOCK), _BF16)),
        grid=(B, nblk),
        in_specs=[row(D_MODEL), row(C_PAD), row(C_PAD)] + [full(w) for w in weights],
        out_specs=a_specs + (row(D_MIX), row(B_WIDTH), row(B_KV_WIDTH), row(B_KV_WIDTH), head4, head4,
                             pl.BlockSpec((1, C_HEADS, ROW_TILE // C_KBLOCK, C_VROWS, C_KBLOCK),
                                          lambda b, i: (b, 0, i, 0, 0))),
        scratch_shapes=[pltpu.VMEM((3 * A_WIDTH // LANES, ROW_TILE, LANES), _F32)],
        compiler_params=pltpu.CompilerParams(dimension_semantics=("arbitrary", "arbitrary"),
                                             vmem_limit_bytes=VMEM_LIMIT),
        name="in_proj",
    )(x, cos_t, sin_t, *weights)
    n_a = 3 * len(A_PATTERNS)
    return [outs[3 * p:3 * p + 3] for p in range(len(A_PATTERNS))], outs[n_a:]


def _dilated_kernel(q_ref, k_ref, v_ref, o_ref, lse_ref, bias_ref, *, length, dilation, group, slopes):
    nblk = length // A_QBLOCK
    lane_head = lax.shift_right_logical(lax.broadcasted_iota(jnp.int32, (A_QBLOCK, A_WIDTH), 1),
                                        int(math.log2(HEAD_DIM)))

    @pl.when((pl.program_id(0) == 0) & (pl.program_id(1) == 0))
    def _():
        rel0 = (lax.broadcasted_iota(jnp.int32, (A_QBLOCK, A_KWIN), 1)
                - lax.broadcasted_iota(jnp.int32, (A_QBLOCK, A_KWIN), 0))
        for case, shift in enumerate((0, -A_HALF, -2 * A_HALF)):
            dist = jnp.abs(rel0 + shift)
            for hd in range(A_HEADS):
                bias_ref[case, hd] = jnp.where(dist <= A_HALF,
                                               dist.astype(_F32) * (-slopes[hd] * dilation * LOG2E), NEG_INF)

    low = lax.broadcasted_iota(jnp.int32, (A_QBLOCK, LANES), 1) < HEAD_DIM

    def load(blk, lanes):
        t0 = pl.multiple_of(blk * A_QBLOCK, A_QBLOCK)
        start = pl.multiple_of(jnp.clip(t0 - A_HALF, 0, length - A_KWIN), A_HALF)
        case = jnp.where(blk == 0, 0, jnp.where(blk == nblk - 1, 2, 1))
        q = q_ref[0, pl.ds(t0, A_QBLOCK), lanes]
        q_heads = jnp.concatenate([jnp.where(lane_head == hd, q, jnp.zeros_like(q)) for hd in range(A_HEADS)],
                                  axis=0)
        return t0, case, q_heads, k_ref[0, pl.ds(start, A_KWIN), lanes], v_ref[0, pl.ds(start, A_KWIN), lanes]

    def softmax(s_all, case):
        ps, ms, ls = [], [], []
        for hd in range(A_HEADS):
            s = s_all[hd * A_QBLOCK:(hd + 1) * A_QBLOCK] + bias_ref[case, hd]
            m = jnp.max(s, axis=-1, keepdims=True)
            p = jnp.exp2(s - m)
            ls.append(jnp.sum(p, axis=-1, keepdims=True))
            ms.append(m)
            ps.append(p.astype(_BF16))
        return ps, ms, ls

    def values(t0, lanes, ps, ms, ls, vw):
        outs, lses = [], []
        for pair in range(A_HEADS // 2):
            h0, h1 = 2 * pair, 2 * pair + 1
            acc = _dot(jnp.concatenate([ps[h0], ps[h1]], axis=0), vw[:, pair * LANES:(pair + 1) * LANES])
            l_pair = jnp.where(low, ls[h0], ls[h1])
            outs.append(jnp.where(low, acc[:A_QBLOCK], acc[A_QBLOCK:]) * (1.0 / l_pair))
            lses.append(jnp.where(low, ms[h0], ms[h1]) + jnp.log2(l_pair))
        o_ref[0, pl.ds(t0, A_QBLOCK), lanes] = jnp.concatenate(outs, axis=-1).astype(o_ref.dtype)
        lse_ref[0, pl.ds(t0, A_QBLOCK), lanes] = jnp.concatenate(lses, axis=-1)

    def two_blocks(i, lanes):
        blocks = [load(2 * i + j, lanes) for j in range(2)]
        scores = [_dot_nt(q_heads, kw) for _, _, q_heads, kw, _ in blocks]
        for (t0, case, _, _, vw), s_all in zip(blocks, scores):
            values(t0, lanes, *softmax(s_all, case), vw)

    def residue_class(lanes):
        def body(i, carry):
            two_blocks(i, lanes)
            return carry
        lax.fori_loop(0, nblk // 2, body, 0, unroll=min(nblk // 2, A_LOOP_UNROLL))

    for g in range(group):
        residue_class(slice(g * A_WIDTH, (g + 1) * A_WIDTH))


def _dilated_pattern(q, k, v, dilation, slopes):
    B, length, _ = q.shape
    group = min(dilation, A_RESIDUES_PER_STEP)
    spec = pl.BlockSpec((1, length, group * A_WIDTH), lambda b, r: (b, 0, r))
    return pl.pallas_call(
        functools.partial(_dilated_kernel, length=length, dilation=dilation, group=group, slopes=slopes),
        out_shape=(jax.ShapeDtypeStruct(q.shape, _BF16), jax.ShapeDtypeStruct(q.shape, _F32)),
        grid=(B, dilation // group),
        in_specs=[spec, spec, spec],
        out_specs=(spec, spec),
        scratch_shapes=[pltpu.VMEM((3, A_HEADS, A_QBLOCK, A_KWIN), _F32)],
        compiler_params=pltpu.CompilerParams(dimension_semantics=("arbitrary", "arbitrary"),
                                             vmem_limit_bytes=VMEM_LIMIT),
        name=f"dilated_d{dilation}",
    )(q, k, v)


def _windowed_kernel(sink_ref, q_ref, k_ref, v_ref, *rest, seq, slopes):
    gate_refs, (o_ref, bias_ref) = rest[:B_HEADS // 2], rest[B_HEADS // 2:]
    nblk = seq // B_QBLOCK

    @pl.when((pl.program_id(0) == 0) & (pl.program_id(1) == 0))
    def _():
        rel0 = (lax.broadcasted_iota(jnp.int32, (B_QBLOCK, B_KWIN), 1)
                - lax.broadcasted_iota(jnp.int32, (B_QBLOCK, B_KWIN), 0))
        for case, shift in enumerate((0, -B_WINDOW, -2 * B_WINDOW)):
            dist = jnp.abs(rel0 + shift)
            for idx, head in enumerate(B_HEAD_ORDER):
                bias_ref[case, idx] = jnp.where(dist <= B_WINDOW, dist.astype(_F32) * (-slopes[head] * LOG2E),
                                                NEG_INF)

    low = lax.broadcasted_iota(jnp.int32, (B_QBLOCK, LANES), 1) < HEAD_DIM
    npair = B_HEADS // 2
    blocks = []
    for j in range(B_QBLOCKS_PER_STEP):
        blk = pl.program_id(1) * B_QBLOCKS_PER_STEP + j
        t0 = pl.multiple_of(blk * B_QBLOCK, B_QBLOCK)
        start = pl.multiple_of(jnp.clip(t0 - B_WINDOW, 0, seq - B_KWIN), B_WINDOW)
        case = jnp.where(blk == 0, 0, jnp.where(blk == nblk - 1, 2, 1))
        rows = slice(j * B_QBLOCK, (j + 1) * B_QBLOCK)
        blocks.append((rows, case, k_ref[0, pl.ds(start, B_KWIN), :], v_ref[0, pl.ds(start, B_KWIN), :]))

    def scores(item):
        (rows, _, kw, _), pair = item
        q = q_ref[0, rows, pair * LANES:(pair + 1) * LANES]
        zero = jnp.zeros_like(q)
        return _dot_nt(jnp.concatenate([jnp.where(low, q, zero), jnp.where(low, zero, q)], axis=0), kw)

    items = [(block, pair) for block in blocks for pair in range(npair)]
    s_next = scores(items[0])
    for n, ((rows, case, _, vw), pair) in enumerate(items):
        s_pair, s_next = s_next, (scores(items[n + 1]) if n + 1 < len(items) else None)
        ps, ms, ls, sinks = [], [], [], []
        for side in range(2):
            idx = 2 * pair + side
            s = s_pair[side * B_QBLOCK:(side + 1) * B_QBLOCK] + bias_ref[case, idx]
            m = jnp.max(s, axis=-1, keepdims=True)
            p = jnp.exp2(s - m)
            ls.append(jnp.sum(p, axis=-1, keepdims=True))
            ms.append(m)
            sinks.append(sink_ref[B_HEAD_ORDER[idx]] * LOG2E)
            ps.append(p.astype(_BF16))
        acc = _dot(jnp.concatenate(ps, axis=0), vw)
        factor = 1.0 / (jnp.where(low, ls[0], ls[1])
                        + jnp.exp2(jnp.where(low, sinks[0], sinks[1]) - jnp.where(low, ms[0], ms[1])))
        out = jnp.where(low, acc[:B_QBLOCK], acc[B_QBLOCK:]) * factor * gate_refs[pair][0, rows, :].astype(_F32)
        o_ref[0, rows, pair * LANES:(pair + 1) * LANES] = out.astype(o_ref.dtype)


def _windowed(qb, kb, vb, gates, sink, slopes):
    B, S, _ = qb.shape
    kv_spec = pl.BlockSpec((1, S, B_KV_WIDTH), lambda b, i, sink: (b, 0, 0))
    q_spec = pl.BlockSpec((1, B_QSTEP, B_WIDTH), lambda b, i, sink: (b, i, 0))
    gate_specs = [pl.BlockSpec((1, B_QSTEP, LANES), lambda b, i, sink, blk=A_WIDTH // LANES + pair: (b, i, blk))
                  for pair in range(B_HEADS // 2)]
    return pl.pallas_call(
        functools.partial(_windowed_kernel, seq=S, slopes=slopes),
        out_shape=jax.ShapeDtypeStruct((B, S, B_WIDTH), _BF16),
        grid_spec=pltpu.PrefetchScalarGridSpec(
            num_scalar_prefetch=1, grid=(B, S // B_QSTEP),
            in_specs=[q_spec, kv_spec, kv_spec] + gate_specs, out_specs=q_spec,
            scratch_shapes=[pltpu.VMEM((3, B_HEADS, B_QBLOCK, B_KWIN), _F32)]),
        compiler_params=pltpu.CompilerParams(dimension_semantics=("arbitrary", "arbitrary"),
                                             vmem_limit_bytes=VMEM_LIMIT),
        name="windowed_gqa",
    )(sink, qb, kb, vb, *[gates] * (B_HEADS // 2))


def _latent_kernel(q_ref, k_ref, vt_ref, o_ref, *, seq):
    nkb = seq // C_KBLOCK
    heads = range(C_PAIR)

    for qb in range(C_QBLOCKS_PER_STEP):
        rows = slice(qb * C_QBLOCK, (qb + 1) * C_QBLOCK)
        qs = [q_ref[0, hd, rows, :] for hd in heads]

        def scores(hd, j):
            return _dot_nt(k_ref[0, hd, j * C_KBLOCK:(j + 1) * C_KBLOCK, :], qs[hd])

        ms = [jnp.full((1, C_QBLOCK), NEG_INF, _F32) for _ in heads]
        accs = [jnp.zeros((C_VROWS, C_QBLOCK), _F32) for _ in heads]
        pending = [[scores(hd, j) for hd in heads] for j in range(C_SCORE_LOOKAHEAD)]
        for j in range(nkb):
            if j + C_SCORE_LOOKAHEAD < nkb:
                pending.append([scores(hd, j + C_SCORE_LOOKAHEAD) for hd in heads])
            s_cur = pending.pop(0)
            for hd in heads:
                m_new = jnp.maximum(ms[hd], jnp.max(s_cur[hd], axis=0, keepdims=True))
                alpha = jnp.exp2(ms[hd] - m_new)
                p = jnp.exp2(s_cur[hd] - m_new).astype(_BF16)
                accs[hd] = alpha * accs[hd] + _dot(vt_ref[0, hd, j], p)
                ms[hd] = m_new
        out_t = jnp.concatenate([accs[hd][:C_V] * (1.0 / accs[hd][C_V:C_V + 1]) for hd in heads], axis=0)
        o_ref[0, rows, :] = out_t.T.astype(o_ref.dtype)


def _latent(qc, kc, vt):
    B, H, S, _ = qc.shape
    nkb = S // C_KBLOCK
    return pl.pallas_call(
        functools.partial(_latent_kernel, seq=S),
        out_shape=jax.ShapeDtypeStruct((B, S, C_WIDTH), _BF16),
        grid=(B, H // C_PAIR, S // C_QSTEP),
        in_specs=[pl.BlockSpec((1, C_PAIR, C_QSTEP, C_PAD), lambda b, g, i: (b, g, i, 0)),
                  pl.BlockSpec((1, C_PAIR, S, C_PAD), lambda b, g, i: (b, g, 0, 0)),
                  pl.BlockSpec((1, C_PAIR, nkb, C_VROWS, C_KBLOCK), lambda b, g, i: (b, g, 0, 0, 0))],
        out_specs=pl.BlockSpec((1, C_QSTEP, C_PAIR * C_V), lambda b, g, i: (b, i, g)),
        compiler_params=pltpu.CompilerParams(dimension_semantics=("arbitrary", "arbitrary", "arbitrary"),
                                             vmem_limit_bytes=VMEM_LIMIT),
        name="latent_attention",
    )(qc, kc, vt)


def _out_kernel(x_ref, g_ref, gc_ref, o1_ref, o2_ref, o3_ref, l1_ref, l2_ref, l3_ref, yb_ref, yc_ref,
                w_o_ref, post_g_ref, out_ref, y_scr, pat_scr):
    outs, lses = [], []
    for p, ((_, d), o_ref, l_ref) in enumerate(zip(A_PATTERNS, (o1_ref, o2_ref, o3_ref), (l1_ref, l2_ref, l3_ref))):
        if d == 1:
            outs.append(o_ref[0].astype(_F32))
            lses.append(l_ref[0])
            continue
        rows = ROW_TILE // d
        nchunk = A_WIDTH // LANES
        for r in range(d):
            for c in range(nchunk):
                lanes = slice(r * A_WIDTH + c * LANES, r * A_WIDTH + (c + 1) * LANES)
                pat_scr[2 * p, c, pl.ds(r, rows, stride=d), :] = o_ref[0, :, lanes].astype(_F32)
                pat_scr[2 * p + 1, c, pl.ds(r, rows, stride=d), :] = l_ref[0, :, lanes]
        outs.append(jnp.concatenate([pat_scr[2 * p, c] for c in range(nchunk)], axis=-1))
        lses.append(jnp.concatenate([pat_scr[2 * p + 1, c] for c in range(nchunk)], axis=-1))
    l1, l2, l3 = lses
    big = jnp.maximum(jnp.maximum(l1, l2), l3)
    e1, e2, e3 = jnp.exp2(l1 - big), jnp.exp2(l2 - big), jnp.exp2(l3 - big)
    num = e1 * outs[0] + e2 * outs[1] + e3 * outs[2]
    ya = num / (e1 + e2 + e3)
    y_scr[:, :A_WIDTH] = (ya * g_ref[0].astype(_F32)).astype(_BF16)
    y_scr[:, A_WIDTH:A_WIDTH + B_WIDTH] = yb_ref[0]
    y_scr[:, A_WIDTH + B_WIDTH:] = yc_ref[0] * gc_ref[0, :, D_MIX // 2 - C_WIDTH:]
    y = _dot(y_scr[...], w_o_ref[0])
    out_ref[0] = x_ref[0] + _rms(y, post_g_ref[0])


def _out_proj(layer, x, gates, o_pats, lse_pats, yb, yc, w_o, post_g):
    B, S, _ = x.shape
    row = lambda n: pl.BlockSpec((1, ROW_TILE, n), lambda b, i: (b, i, 0))
    full = lambda a: _layer_block(a, layer)
    return pl.pallas_call(
        _out_kernel,
        out_shape=jax.ShapeDtypeStruct(x.shape, x.dtype),
        grid=(B, S // ROW_TILE),
        in_specs=[row(D_MODEL), row(A_WIDTH), pl.BlockSpec((1, ROW_TILE, D_MIX // 2), lambda b, i: (b, i, 1))]
                 + [_residue_view_rows(d) for _, d in A_PATTERNS] * 2
                 + [row(B_WIDTH), row(C_WIDTH), full(w_o), full(post_g)],
        out_specs=row(D_MODEL),
        scratch_shapes=[pltpu.VMEM((ROW_TILE, D_MIX), _BF16),
                        pltpu.VMEM((2 * len(A_PATTERNS), A_WIDTH // LANES, ROW_TILE, LANES), _F32)],
        compiler_params=pltpu.CompilerParams(dimension_semantics=("arbitrary", "arbitrary"),
                                             vmem_limit_bytes=VMEM_LIMIT),
        name="out_proj",
    )(x, gates, gates, *o_pats, *lse_pats, yb, yc, w_o, post_g)


def _pair_heads(t, axis):
    parts = jnp.split(t, B_HEADS, axis=axis)
    return jnp.concatenate([parts[h] for h in B_HEAD_ORDER], axis=axis)


def _rot_cols(w):
    half = C_ROPE // 2
    return jnp.concatenate([-w[..., half:], w[..., :half]], axis=-1)


def _stacked_weights(w_in, w_uq, w_ukv, w_o):
    depth = w_in.shape[0]
    o = 0
    cols = {}
    for name, n in (("qa", A_WIDTH), ("ka", A_WIDTH), ("va", A_WIDTH), ("ga", A_WIDTH),
                    ("qb", B_WIDTH), ("kb", B_KV_WIDTH), ("vb", B_KV_WIDTH), ("gb", B_WIDTH),
                    ("cq", Q_LORA), ("ckv", KV_LORA), ("kr", C_ROPE), ("gc", C_WIDTH)):
        cols[name] = w_in[..., o:o + n]
        o += n
    w_a = jnp.concatenate([cols["qa"], cols["ka"], cols["va"]], axis=-1)
    w_g = jnp.concatenate([cols["ga"], _pair_heads(cols["gb"], -1), cols["gc"]], axis=-1)
    w_b = jnp.concatenate([_pair_heads(cols["qb"], -1), cols["kb"], cols["vb"]], axis=-1)
    w_c = jnp.concatenate([cols["cq"], cols["ckv"], jnp.zeros((depth, D_MODEL, C_NOPE), w_in.dtype),
                           cols["kr"], _rot_cols(cols["kr"])], axis=-1)
    uq = w_uq.reshape(depth, Q_LORA, C_HEADS, C_NOPE + C_ROPE)
    uq = jnp.concatenate([uq, _rot_cols(uq[..., C_NOPE:])], axis=-1).reshape(depth, Q_LORA, C_HEADS * C_PAD)
    ukv = w_ukv.reshape(depth, KV_LORA, C_HEADS, C_NOPE + C_V)
    uk = jnp.concatenate([ukv[..., :C_NOPE], jnp.zeros((depth, KV_LORA, C_HEADS, C_PAD - C_NOPE), w_ukv.dtype)],
                         axis=-1).reshape(depth, KV_LORA, C_HEADS * C_PAD)
    uvt = jnp.transpose(ukv[..., C_NOPE:], (0, 2, 3, 1))
    uvt = jnp.concatenate([uvt, jnp.zeros((depth, C_HEADS, C_VROWS - C_V, KV_LORA), w_ukv.dtype)], axis=2)
    uvt = uvt.reshape(depth, C_HEADS * C_VROWS, KV_LORA)

    w_o_p = jnp.concatenate([w_o[:, :A_WIDTH], _pair_heads(w_o[:, A_WIDTH:A_WIDTH + B_WIDTH], 1),
                             w_o[:, A_WIDTH + B_WIDTH:]], axis=1)
    bf = lambda t: t.astype(_BF16)
    return (bf(w_a), bf(w_g), bf(w_b), bf(w_c), bf(uq), bf(uk), bf(uvt)), bf(w_o_p)


def kernel(x, positions, pre_norm, w_in, q_a_norm, kv_a_norm, w_uq, w_ukv, sink, w_o, post_norm):
    depth = w_in.shape[0]
    slopes_a, slopes_b = _alibi_slopes()
    cos_t, sin_t = _rope_tables(positions)
    in_w, w_o_p = _stacked_weights(w_in, w_uq, w_ukv, w_o)
    gains = [g[:, None, :] for g in (pre_norm, q_a_norm, kv_a_norm, post_norm)]
    for i in range(depth):
        qkv_a, (gates, qb, kb, vb, qc, kc, vt) = _in_proj(i, x, cos_t, sin_t, *gains[:3], *in_w)
        pats = [_dilated_pattern(*qkv, d, slopes_a) for qkv, (_, d) in zip(qkv_a, A_PATTERNS)]
        yb = _windowed(qb, kb, vb, gates, sink[i], slopes_b)
        yc = _latent(qc, kc, vt)
        x = _out_proj(i, x, gates, [p[0] for p in pats], [p[1] for p in pats], yb, yc, w_o_p, gains[3])
    return x
```

```python
import functools
import math

import numpy as np
import jax
import jax.numpy as jnp
from jax import lax
from jax.experimental import pallas as pl
from jax.experimental.pallas import tpu as pltpu

D_MODEL = 1024
HEAD_DIM = 64
A_HEADS = 4
A_WIDTH = A_HEADS * HEAD_DIM
A_PATTERNS = ((128, 1), (512, 4), (2048, 16))
A_HALF = 64
B_HEADS = 6
B_KV_HEADS = 2
B_GROUP = B_HEADS // B_KV_HEADS
B_WIDTH = B_HEADS * HEAD_DIM
B_KV_WIDTH = B_KV_HEADS * HEAD_DIM
B_WINDOW = 128
C_HEADS = 6
C_NOPE = 64
C_ROPE = 32
C_V = 64
C_WIDTH = C_HEADS * C_V
Q_LORA = 256
KV_LORA = 128
ROPE_THETA = 10000.0
D_MIX = A_WIDTH + B_WIDTH + C_WIDTH
N_ALIBI = A_HEADS + B_HEADS
RMS_EPS = 1e-6
NEG_INF = -1e30
LOG2E = math.log2(math.e)

LANES = 128
C_PAD = LANES
C_VROWS = 80
C_PAIR = 2

ROW_TILE = 1024
A_QBLOCK = 128
A_KWIN = A_QBLOCK + 2 * A_HALF
A_RESIDUES_PER_STEP = 8
A_LOOP_UNROLL = 8
B_QBLOCK = 128
B_KWIN = B_QBLOCK + 2 * B_WINDOW
B_QBLOCKS_PER_STEP = 16
B_QSTEP = B_QBLOCKS_PER_STEP * B_QBLOCK
C_QBLOCK = 512
C_KBLOCK = 256
C_QBLOCKS_PER_STEP = 4
C_SCORE_LOOKAHEAD = 1
C_QSTEP = C_QBLOCKS_PER_STEP * C_QBLOCK
VMEM_LIMIT = 48 * 1024 * 1024

B_HEAD_ORDER = (0, 3, 1, 4, 2, 5)

_F32 = jnp.float32
_BF16 = jnp.bfloat16


def _alibi_slopes():
    s = 2.0 ** (-8.0 * np.arange(1, N_ALIBI + 1, dtype=np.float64) / N_ALIBI)
    return [float(v) for v in s[B_HEADS:]], [float(v) for v in s[:B_HEADS]]


def _rms(x, g):
    return x * lax.rsqrt(jnp.mean(x * x, axis=-1, keepdims=True) + RMS_EPS) * g


def _residue_view_shape(batch, seq, d, dtype):
    return jax.ShapeDtypeStruct((batch, seq // d, d * A_WIDTH), dtype)


def _layer_block(stacked, layer):
    return pl.BlockSpec((1,) + stacked.shape[1:], lambda b, i: (layer,) + (0,) * (stacked.ndim - 1))


def _residue_view_rows(d):
    return pl.BlockSpec((1, ROW_TILE // d, d * A_WIDTH), lambda b, i: (b, i, 0))


def _dot(a, b):
    return jnp.dot(a, b, preferred_element_type=_F32)


def _dot_nt(a, b):
    return lax.dot_general(a, b, (((1,), (1,)), ((), ())), preferred_element_type=_F32)


def _rope_table_kernel(pos_ref, freq_ref, cos_ref, sin_ref):
    ang = freq_ref[...] * pos_ref[0].astype(_F32)
    cos, sin = jnp.cos(ang), jnp.sin(ang)
    tokens = ang.shape[1]
    ones = jnp.ones((C_NOPE, tokens), _F32)
    zeros = jnp.zeros((C_NOPE, tokens), _F32)
    tail = zeros[:C_PAD - C_NOPE - C_ROPE]
    cos_ref[0] = jnp.concatenate([ones, cos, cos, tail], axis=0).T
    sin_ref[0] = jnp.concatenate([zeros, sin, sin, tail], axis=0).T


def _rope_tables(positions):
    B, S = positions.shape
    half = C_ROPE // 2
    freq = ROPE_THETA ** (-2.0 * jnp.arange(half, dtype=_F32) / C_ROPE)
    out = jax.ShapeDtypeStruct((B, S, C_PAD), _F32)
    return pl.pallas_call(
        _rope_table_kernel,
        out_shape=(out, out),
        grid=(B, S // ROW_TILE),
        in_specs=[pl.BlockSpec((1, 1, ROW_TILE), lambda b, i: (b, 0, i)),
                  pl.BlockSpec((half, 1), lambda b, i: (0, 0))],
        out_specs=(pl.BlockSpec((1, ROW_TILE, C_PAD), lambda b, i: (b, i, 0)),
                   pl.BlockSpec((1, ROW_TILE, C_PAD), lambda b, i: (b, i, 0))),
        compiler_params=pltpu.CompilerParams(dimension_semantics=("arbitrary", "arbitrary")),
        name="rope_tables",
    )(positions.reshape(B, 1, S), freq.reshape(half, 1))


def _in_proj_kernel(x_ref, cos_ref, sin_ref, pre_g_ref, qn_g_ref, kvn_g_ref,
                    w_a_ref, w_g_ref, w_b_ref, w_c_ref, w_uq_ref, w_uk_ref, w_uvt_ref,
                    *rest):
    a_refs, (g_ref, qb_ref, kb_ref, vb_ref, qc_ref, kc_ref, vt_ref, a_scr) = rest[:3 * len(A_PATTERNS)], rest[-8:]
    h = _rms(x_ref[0], pre_g_ref[0]).astype(_BF16)
    score_scale = HEAD_DIM ** -0.5 * LOG2E

    pa = _dot(h, w_a_ref[0])
    nchunk = 3 * A_WIDTH // LANES
    for c in range(nchunk):
        chunk = pa[:, c * LANES:(c + 1) * LANES]
        a_scr[c] = chunk * score_scale if c < A_WIDTH // LANES else chunk
    for p, (_, d) in enumerate(A_PATTERNS):
        rows = ROW_TILE // d
        for r in range(d):
            for c in range(nchunk):
                res = a_scr[c, pl.ds(r, rows, stride=d), :] if d > 1 else a_scr[c]
                lane0 = r * A_WIDTH + (c * LANES) % A_WIDTH
                a_refs[3 * p + c * LANES // A_WIDTH][0, :, lane0:lane0 + LANES] = res.astype(_BF16)

    for c in range(0, D_MIX, 256):
        g = _dot(h, w_g_ref[0, :, c:c + 256])
        g_ref[0, :, c:c + 256] = (g / (1.0 + jnp.exp(-g))).astype(_BF16)

    pb = _dot(h, w_b_ref[0])
    qb_ref[0] = (pb[:, :B_WIDTH] * score_scale).astype(_BF16)
    kb_ref[0] = pb[:, B_WIDTH:B_WIDTH + B_KV_WIDTH].astype(_BF16)
    vb_ref[0] = pb[:, B_WIDTH + B_KV_WIDTH:].astype(_BF16)

    pc = _dot(h, w_c_ref[0])
    cos = cos_ref[0]
    sin = sin_ref[0]

    def rotary(t):
        return t * cos + pltpu.roll(t, C_PAD - C_ROPE, axis=1) * sin

    cq = _rms(pc[:, :Q_LORA], qn_g_ref[0]).astype(_BF16)
    ckv = _rms(pc[:, Q_LORA:Q_LORA + KV_LORA], kvn_g_ref[0]).astype(_BF16)
    k_rope = rotary(pc[:, Q_LORA + KV_LORA:])
    c_scale = (C_NOPE + C_ROPE) ** -0.5 * LOG2E
    for pair in range(C_HEADS // 2):
        cols = slice(2 * pair * C_PAD, 2 * (pair + 1) * C_PAD)
        q_pair = _dot(cq, w_uq_ref[0, :, cols])
        k_pair = _dot(ckv, w_uk_ref[0, :, cols])
        for side in range(2):
            lanes = slice(side * C_PAD, (side + 1) * C_PAD)
            qc_ref[0, 2 * pair + side] = (rotary(q_pair[:, lanes]) * c_scale).astype(_BF16)
            kc_ref[0, 2 * pair + side] = (k_pair[:, lanes] + k_rope).astype(_BF16)
    vt = _dot_nt(w_uvt_ref[0], ckv)
    row = lax.broadcasted_iota(jnp.int32, vt.shape, 0)
    ones_row = functools.reduce(jnp.logical_or, [row == hd * C_VROWS + C_V for hd in range(C_HEADS)])
    vt = jnp.where(ones_row, 1.0, vt).astype(_BF16)
    for hd in range(C_HEADS):
        for c in range(ROW_TILE // C_KBLOCK):
            vt_ref[0, hd, c] = vt[hd * C_VROWS:(hd + 1) * C_VROWS, c * C_KBLOCK:(c + 1) * C_KBLOCK]


def _in_proj(layer, x, cos_t, sin_t, pre_g, qn_g, kvn_g, w_a, w_g, w_b, w_c, w_uq, w_uk, w_uvt):
    B, S, _ = x.shape
    nblk = S // ROW_TILE
    row = lambda n: pl.BlockSpec((1, ROW_TILE, n), lambda b, i: (b, i, 0))
    full = lambda a: _layer_block(a, layer)
    bf = lambda n: jax.ShapeDtypeStruct((B, S, n), _BF16)
    head4 = pl.BlockSpec((1, C_HEADS, ROW_TILE, C_PAD), lambda b, i: (b, 0, i, 0))
    weights = (pre_g, qn_g, kvn_g, w_a, w_g, w_b, w_c, w_uq, w_uk, w_uvt)
    a_shapes = tuple(_residue_view_shape(B, S, d, _BF16) for _, d in A_PATTERNS for _ in range(3))
    a_specs = tuple(_residue_view_rows(d) for _, d in A_PATTERNS for _ in range(3))
    outs = pl.pallas_call(
        _in_proj_kernel,
        out_shape=a_shapes + (bf(D_MIX), bf(B_WIDTH), bf(B_KV_WIDTH), bf(B_KV_WIDTH),
                              jax.ShapeDtypeStruct((B, C_HEADS, S, C_PAD), _BF16),
                              jax.ShapeDtypeStruct((B, C_HEADS, S, C_PAD), _BF16),
                              jax.ShapeDtypeStruct((B, C_HEADS, S // C_KBLOCK, C_VROWS, C_KBLOCK), _BF16)),
        grid=(B, nblk),
        in_specs=[row(D_MODEL), row(C_PAD), row(C_PAD)] + [full(w) for w in weights],
        out_specs=a_specs + (row(D_MIX), row(B_WIDTH), row(B_KV_WIDTH), row(B_KV_WIDTH), head4, head4,
                             pl.BlockSpec((1, C_HEADS, ROW_TILE // C_KBLOCK, C_VROWS, C_KBLOCK),
                                          lambda b, i: (b, 0, i, 0, 0))),
        scratch_shapes=[pltpu.VMEM((3 * A_WIDTH // LANES, ROW_TILE, LANES), _F32)],
        compiler_params=pltpu.CompilerParams(dimension_semantics=("arbitrary", "arbitrary"),
                                             vmem_limit_bytes=VMEM_LIMIT),
        name="in_proj",
    )(x, cos_t, sin_t, *weights)
    n_a = 3 * len(A_PATTERNS)
    return [outs[3 * p:3 * p + 3] for p in range(len(A_PATTERNS))], outs[n_a:]


def _dilated_kernel(q_ref, k_ref, v_ref, o_ref, lse_ref, bias_ref, *, length, dilation, group, slopes):
    nblk = length // A_QBLOCK
    lane_head = lax.shift_right_logical(lax.broadcasted_iota(jnp.int32, (A_QBLOCK, A_WIDTH), 1),
                                        int(math.log2(HEAD_DIM)))

    @pl.when((pl.program_id(0) == 0) & (pl.program_id(1) == 0))
    def _():
        rel0 = (lax.broadcasted_iota(jnp.int32, (A_QBLOCK, A_KWIN), 1)
                - lax.broadcasted_iota(jnp.int32, (A_QBLOCK, A_KWIN), 0))
        for case, shift in enumerate((0, -A_HALF, -2 * A_HALF)):
            dist = jnp.abs(rel0 + shift)
            for hd in range(A_HEADS):
                bias_ref[case, hd] = jnp.where(dist <= A_HALF,
                                               dist.astype(_F32) * (-slopes[hd] * dilation * LOG2E), NEG_INF)

    low = lax.broadcasted_iota(jnp.int32, (A_QBLOCK, LANES), 1) < HEAD_DIM

    def load(blk, lanes):
        t0 = pl.multiple_of(blk * A_QBLOCK, A_QBLOCK)
        start = pl.multiple_of(jnp.clip(t0 - A_HALF, 0, length - A_KWIN), A_HALF)
        case = jnp.where(blk == 0, 0, jnp.where(blk == nblk - 1, 2, 1))
        q = q_ref[0, pl.ds(t0, A_QBLOCK), lanes]
        q_heads = jnp.concatenate([jnp.where(lane_head == hd, q, jnp.zeros_like(q)) for hd in range(A_HEADS)],
                                  axis=0)
        return t0, case, q_heads, k_ref[0, pl.ds(start, A_KWIN), lanes], v_ref[0, pl.ds(start, A_KWIN), lanes]

    def softmax(s_all, case):
        ps, ms, ls = [], [], []
        for hd in range(A_HEADS):
            s = s_all[hd * A_QBLOCK:(hd + 1) * A_QBLOCK] + bias_ref[case, hd]
            m = jnp.max(s, axis=-1, keepdims=True)
            p = jnp.exp2(s - m)
            ls.append(jnp.sum(p, axis=-1, keepdims=True))
            ms.append(m)
            ps.append(p.astype(_BF16))
        return ps, ms, ls

    def values(t0, lanes, ps, ms, ls, vw):
        outs, lses = [], []
        for pair in range(A_HEADS // 2):
            h0, h1 = 2 * pair, 2 * pair + 1
            acc = _dot(jnp.concatenate([ps[h0], ps[h1]], axis=0), vw[:, pair * LANES:(pair + 1) * LANES])
            l_pair = jnp.where(low, ls[h0], ls[h1])
            outs.append(jnp.where(low, acc[:A_QBLOCK], acc[A_QBLOCK:]) * (1.0 / l_pair))
            lses.append(jnp.where(low, ms[h0], ms[h1]) + jnp.log2(l_pair))
        o_ref[0, pl.ds(t0, A_QBLOCK), lanes] = jnp.concatenate(outs, axis=-1).astype(o_ref.dtype)
        lse_ref[0, pl.ds(t0, A_QBLOCK), lanes] = jnp.concatenate(lses, axis=-1)

    def two_blocks(i, lanes):
        blocks = [load(2 * i + j, lanes) for j in range(2)]
        scores = [_dot_nt(q_heads, kw) for _, _, q_heads, kw, _ in blocks]
        for (t0, case, _, _, vw), s_all in zip(blocks, scores):
            values(t0, lanes, *softmax(s_all, case), vw)

    def residue_class(lanes):
        def body(i, carry):
            two_blocks(i, lanes)
            return carry
        lax.fori_loop(0, nblk // 2, body, 0, unroll=min(nblk // 2, A_LOOP_UNROLL))

    for g in range(group):
        residue_class(slice(g * A_WIDTH, (g + 1) * A_WIDTH))


def _dilated_pattern(q, k, v, dilation, slopes):
    B, length, _ = q.shape
    group = min(dilation, A_RESIDUES_PER_STEP)
    spec = pl.BlockSpec((1, length, group * A_WIDTH), lambda b, r: (b, 0, r))
    return pl.pallas_call(
        functools.partial(_dilated_kernel, length=length, dilation=dilation, group=group, slopes=slopes),
        out_shape=(jax.ShapeDtypeStruct(q.shape, _BF16), jax.ShapeDtypeStruct(q.shape, _F32)),
        grid=(B, dilation // group),
        in_specs=[spec, spec, spec],
        out_specs=(spec, spec),
        scratch_shapes=[pltpu.VMEM((3, A_HEADS, A_QBLOCK, A_KWIN), _F32)],
        compiler_params=pltpu.CompilerParams(dimension_semantics=("arbitrary", "arbitrary"),
                                             vmem_limit_bytes=VMEM_LIMIT),
        name=f"dilated_d{dilation}",
    )(q, k, v)


def _windowed_kernel(sink_ref, q_ref, k_ref, v_ref, *rest, seq, slopes):
    gate_refs, (o_ref, bias_ref) = rest[:B_HEADS // 2], rest[B_HEADS // 2:]
    nblk = seq // B_QBLOCK

    @pl.when((pl.program_id(0) == 0) & (pl.program_id(1) == 0))
    def _():
        rel0 = (lax.broadcasted_iota(jnp.int32, (B_QBLOCK, B_KWIN), 1)
                - lax.broadcasted_iota(jnp.int32, (B_QBLOCK, B_KWIN), 0))
        for case, shift in enumerate((0, -B_WINDOW, -2 * B_WINDOW)):
            dist = jnp.abs(rel0 + shift)
            for idx, head in enumerate(B_HEAD_ORDER):
                bias_ref[case, idx] = jnp.where(dist <= B_WINDOW, dist.astype(_F32) * (-slopes[head] * LOG2E),
                                                NEG_INF)

    low = lax.broadcasted_iota(jnp.int32, (B_QBLOCK, LANES), 1) < HEAD_DIM
    npair = B_HEADS // 2
    blocks = []
    for j in range(B_QBLOCKS_PER_STEP):
        blk = pl.program_id(1) * B_QBLOCKS_PER_STEP + j
        t0 = pl.multiple_of(blk * B_QBLOCK, B_QBLOCK)
        start = pl.multiple_of(jnp.clip(t0 - B_WINDOW, 0, seq - B_KWIN), B_WINDOW)
        case = jnp.where(blk == 0, 0, jnp.where(blk == nblk - 1, 2, 1))
        rows = slice(j * B_QBLOCK, (j + 1) * B_QBLOCK)
        blocks.append((rows, case, k_ref[0, pl.ds(start, B_KWIN), :], v_ref[0, pl.ds(start, B_KWIN), :]))

    def scores(item):
        (rows, _, kw, _), pair = item
        q = q_ref[0, rows, pair * LANES:(pair + 1) * LANES]
        zero = jnp.zeros_like(q)
        return _dot_nt(jnp.concatenate([jnp.where(low, q, zero), jnp.where(low, zero, q)], axis=0), kw)

    items = [(block, pair) for block in blocks for pair in range(npair)]
    s_next = scores(items[0])
    for n, ((rows, case, _, vw), pair) in enumerate(items):
        s_pair, s_next = s_next, (scores(items[n + 1]) if n + 1 < len(items) else None)
        ps, ms, ls, sinks = [], [], [], []
        for side in range(2):
            idx = 2 * pair + side
            s = s_pair[side * B_QBLOCK:(side + 1) * B_QBLOCK] + bias_ref[case, idx]
            m = jnp.max(s, axis=-1, keepdims=True)
            p = jnp.exp2(s - m)
            ls.append(jnp.sum(p, axis=-1, keepdims=True))
            ms.append(m)
            sinks.append(sink_ref[B_HEAD_ORDER[idx]] * LOG2E)
            ps.append(p.astype(_BF16))
        acc = _dot(jnp.concatenate(ps, axis=0), vw)
        factor = 1.0 / (jnp.where(low, ls[0], ls[1])
                        + jnp.exp2(jnp.where(low, sinks[0], sinks[1]) - jnp.where(low, ms[0], ms[1])))
        out = jnp.where(low, acc[:B_QBLOCK], acc[B_QBLOCK:]) * factor * gate_refs[pair][0, rows, :].astype(_F32)
        o_ref[0, rows, pair * LANES:(pair + 1) * LANES] = out.astype(o_ref.dtype)


def _windowed(qb, kb, vb, gates, sink, slopes):
    B, S, _ = qb.shape
    kv_spec = pl.BlockSpec((1, S, B_KV_WIDTH), lambda b, i, sink: (b, 0, 0))
    q_spec = pl.BlockSpec((1, B_QSTEP, B_WIDTH), lambda b, i, sink: (b, i, 0))
    gate_specs = [pl.BlockSpec((1, B_QSTEP, LANES), lambda b, i, sink, blk=A_WIDTH // LANES + pair: (b, i, blk))
                  for pair in range(B_HEADS // 2)]
    return pl.pallas_call(
        functools.partial(_windowed_kernel, seq=S, slopes=slopes),
        out_shape=jax.ShapeDtypeStruct((B, S, B_WIDTH), _BF16),
        grid_spec=pltpu.PrefetchScalarGridSpec(
            num_scalar_prefetch=1, grid=(B, S // B_QSTEP),
            in_specs=[q_spec, kv_spec, kv_spec] + gate_specs, out_specs=q_spec,
            scratch_shapes=[pltpu.VMEM((3, B_HEADS, B_QBLOCK, B_KWIN), _F32)]),
        compiler_params=pltpu.CompilerParams(dimension_semantics=("arbitrary", "arbitrary"),
                                             vmem_limit_bytes=VMEM_LIMIT),
        name="windowed_gqa",
    )(sink, qb, kb, vb, *[gates] * (B_HEADS // 2))


def _latent_kernel(q_ref, k_ref, vt_ref, o_ref, *, seq):
    nkb = seq // C_KBLOCK
    heads = range(C_PAIR)

    for qb in range(C_QBLOCKS_PER_STEP):
        rows = slice(qb * C_QBLOCK, (qb + 1) * C_QBLOCK)
        qs = [q_ref[0, hd, rows, :] for hd in heads]

        def scores(hd, j):
            return _dot_nt(k_ref[0, hd, j * C_KBLOCK:(j + 1) * C_KBLOCK, :], qs[hd])

        ms = [jnp.full((1, C_QBLOCK), NEG_INF, _F32) for _ in heads]
        accs = [jnp.zeros((C_VROWS, C_QBLOCK), _F32) for _ in heads]
        pending = [[scores(hd, j) for hd in heads] for j in range(C_SCORE_LOOKAHEAD)]
        for j in range(nkb):
            if j + C_SCORE_LOOKAHEAD < nkb:
                pending.append([scores(hd, j + C_SCORE_LOOKAHEAD) for hd in heads])
            s_cur = pending.pop(0)
            for hd in heads:
                m_new = jnp.maximum(ms[hd], jnp.max(s_cur[hd], axis=0, keepdims=True))
                alpha = jnp.exp2(ms[hd] - m_new)
                p = jnp.exp2(s_cur[hd] - m_new).astype(_BF16)
                accs[hd] = alpha * accs[hd] + _dot(vt_ref[0, hd, j], p)
                ms[hd] = m_new
        out_t = jnp.concatenate([accs[hd][:C_V] * (1.0 / accs[hd][C_V:C_V + 1]) for hd in heads], axis=0)
        o_ref[0, rows, :] = out_t.T.astype(o_ref.dtype)


def _latent(qc, kc, vt):
    B, H, S, _ = qc.shape
    nkb = S // C_KBLOCK
    return pl.pallas_call(
        functools.partial(_latent_kernel, seq=S),
        out_shape=jax.ShapeDtypeStruct((B, S, C_WIDTH), _BF16),
        grid=(B, H // C_PAIR, S // C_QSTEP),
        in_specs=[pl.BlockSpec((1, C_PAIR, C_QSTEP, C_PAD), lambda b, g, i: (b, g, i, 0)),
                  pl.BlockSpec((1, C_PAIR, S, C_PAD), lambda b, g, i: (b, g, 0, 0)),
                  pl.BlockSpec((1, C_PAIR, nkb, C_VROWS, C_KBLOCK), lambda b, g, i: (b, g, 0, 0, 0))],
        out_specs=pl.BlockSpec((1, C_QSTEP, C_PAIR * C_V), lambda b, g, i: (b, i, g)),
        compiler_params=pltpu.CompilerParams(dimension_semantics=("arbitrary", "arbitrary", "arbitrary"),
                                             vmem_limit_bytes=VMEM_LIMIT),
        name="latent_attention",
    )(qc, kc, vt)


def _out_kernel(x_ref, g_ref, gc_ref, o1_ref, o2_ref, o3_ref, l1_ref, l2_ref, l3_ref, yb_ref, yc_ref,
                w_o_ref, post_g_ref, out_ref, y_scr, pat_scr):
    outs, lses = [], []
    for p, ((_, d), o_ref, l_ref) in enumerate(zip(A_PATTERNS, (o1_ref, o2_ref, o3_ref), (l1_ref, l2_ref, l3_ref))):
        if d == 1:
            outs.append(o_ref[0].astype(_F32))
            lses.append(l_ref[0])
            continue
        rows = ROW_TILE // d
        nchunk = A_WIDTH // LANES
        for r in range(d):
            for c in range(nchunk):
                lanes = slice(r * A_WIDTH + c * LANES, r * A_WIDTH + (c + 1) * LANES)
                pat_scr[2 * p, c, pl.ds(r, rows, stride=d), :] = o_ref[0, :, lanes].astype(_F32)
                pat_scr[2 * p + 1, c, pl.ds(r, rows, stride=d), :] = l_ref[0, :, lanes]
        outs.append(jnp.concatenate([pat_scr[2 * p, c] for c in range(nchunk)], axis=-1))
        lses.append(jnp.concatenate([pat_scr[2 * p + 1, c] for c in range(nchunk)], axis=-1))
    l1, l2, l3 = lses
    big = jnp.maximum(jnp.maximum(l1, l2), l3)
    e1, e2, e3 = jnp.exp2(l1 - big), jnp.exp2(l2 - big), jnp.exp2(l3 - big)
    num = e1 * outs[0] + e2 * outs[1] + e3 * outs[2]
    ya = num / (e1 + e2 + e3)
    y_scr[:, :A_WIDTH] = (ya * g_ref[0].astype(_F32)).astype(_BF16)
    y_scr[:, A_WIDTH:A_WIDTH + B_WIDTH] = yb_ref[0]
    y_scr[:, A_WIDTH + B_WIDTH:] = yc_ref[0] * gc_ref[0, :, D_MIX // 2 - C_WIDTH:]
    y = _dot(y_scr[...], w_o_ref[0])
    out_ref[0] = x_ref[0] + _rms(y, post_g_ref[0])


def _out_proj(layer, x, gates, o_pats, lse_pats, yb, yc, w_o, post_g):
    B, S, _ = x.shape
    row = lambda n: pl.BlockSpec((1, ROW_TILE, n), lambda b, i: (b, i, 0))
    full = lambda a: _layer_block(a, layer)
    return pl.pallas_call(
        _out_kernel,
        out_shape=jax.ShapeDtypeStruct(x.shape, x.dtype),
        grid=(B, S // ROW_TILE),
        in_specs=[row(D_MODEL), row(A_WIDTH), pl.BlockSpec((1, ROW_TILE, D_MIX // 2), lambda b, i: (b, i, 1))]
                 + [_residue_view_rows(d) for _, d in A_PATTERNS] * 2
                 + [row(B_WIDTH), row(C_WIDTH), full(w_o), full(post_g)],
        out_specs=row(D_MODEL),
        scratch_shapes=[pltpu.VMEM((ROW_TILE, D_MIX), _BF16),
                        pltpu.VMEM((2 * len(A_PATTERNS), A_WIDTH // LANES, ROW_TILE, LANES), _F32)],
        compiler_params=pltpu.CompilerParams(dimension_semantics=("arbitrary", "arbitrary"),
                                             vmem_limit_bytes=VMEM_LIMIT),
        name="out_proj",
    )(x, gates, gates, *o_pats, *lse_pats, yb, yc, w_o, post_g)


def _pair_heads(t, axis):
    parts = jnp.split(t, B_HEADS, axis=axis)
    return jnp.concatenate([parts[h] for h in B_HEAD_ORDER], axis=axis)


def _rot_cols(w):
    half = C_ROPE // 2
    return jnp.concatenate([-w[..., half:], w[..., :half]], axis=-1)


def _stacked_weights(w_in, w_uq, w_ukv, w_o):
    depth = w_in.shape[0]
    o = 0
    cols = {}
    for name, n in (("qa", A_WIDTH), ("ka", A_WIDTH), ("va", A_WIDTH), ("ga", A_WIDTH),
                    ("qb", B_WIDTH), ("kb", B_KV_WIDTH), ("vb", B_KV_WIDTH), ("gb", B_WIDTH),
                    ("cq", Q_LORA), ("ckv", KV_LORA), ("kr", C_ROPE), ("gc", C_WIDTH)):
        cols[name] = w_in[..., o:o + n]
        o += n
    w_a = jnp.concatenate([cols["qa"], cols["ka"], cols["va"]], axis=-1)
    w_g = jnp.concatenate([cols["ga"], _pair_heads(cols["gb"], -1), cols["gc"]], axis=-1)
    w_b = jnp.concatenate([_pair_heads(cols["qb"], -1), cols["kb"], cols["vb"]], axis=-1)
    w_c = jnp.concatenate([cols["cq"], cols["ckv"], jnp.zeros((depth, D_MODEL, C_NOPE), w_in.dtype),
                           cols["kr"], _rot_cols(cols["kr"])], axis=-1)
    uq = w_uq.reshape(depth, Q_LORA, C_HEADS, C_NOPE + C_ROPE)
    uq = jnp.concatenate([uq, _rot_cols(uq[..., C_NOPE:])], axis=-1).reshape(depth, Q_LORA, C_HEADS * C_PAD)
    ukv = w_ukv.reshape(depth, KV_LORA, C_HEADS, C_NOPE + C_V)
    uk = jnp.concatenate([ukv[..., :C_NOPE], jnp.zeros((depth, KV_LORA, C_HEADS, C_PAD - C_NOPE), w_ukv.dtype)],
                         axis=-1).reshape(depth, KV_LORA, C_HEADS * C_PAD)
    uvt = jnp.transpose(ukv[..., C_NOPE:], (0, 2, 3, 1))
    uvt = jnp.concatenate([uvt, jnp.zeros((depth, C_HEADS, C_VROWS - C_V, KV_LORA), w_ukv.dtype)], axis=2)
    uvt = uvt.reshape(depth, C_HEADS * C_VROWS, KV_LORA)

    w_o_p = jnp.concatenate([w_o[:, :A_WIDTH], _pair_heads(w_o[:, A_WIDTH:A_WIDTH + B_WIDTH], 1),
                             w_o[:, A_WIDTH + B_WIDTH:]], axis=1)
    bf = lambda t: t.astype(_BF16)
    return (bf(w_a), bf(w_g), bf(w_b), bf(w_c), bf(uq), bf(uk), bf(uvt)), bf(w_o_p)


def kernel(x, positions, pre_norm, w_in, q_a_norm, kv_a_norm, w_uq, w_ukv, sink, w_o, post_norm):
    depth = w_in.shape[0]
    slopes_a, slopes_b = _alibi_slopes()
    cos_t, sin_t = _rope_tables(positions)
    in_w, w_o_p = _stacked_weights(w_in, w_uq, w_ukv, w_o)
    gains = [g[:, None, :] for g in (pre_norm, q_a_norm, kv_a_norm, post_norm)]
    for i in range(depth):
        qkv_a, (gates, qb, kb, vb, qc, kc, vt) = _in_proj(i, x, cos_t, sin_t, *gains[:3], *in_w)
        pats = [_dilated_pattern(*qkv, d, slopes_a) for qkv, (_, d) in zip(qkv_a, A_PATTERNS)]
        yb = _windowed(qb, kb, vb, gates, sink[i], slopes_b)
        yc = _latent(qc, kc, vt)
        x = _out_proj(i, x, gates, [p[0] for p in pats], [p[1] for p in pats], yb, yc, w_o_p, gains[3])
    return x
```
